```python
import jax, jax.numpy as jnp
from jax import lax
import numpy as np

D_MODEL = 1024
BATCH = 4
SEQ = 4096
DEPTH = 4

N_MIXERS = 2
N_LAYERS_A = (DEPTH + 1) // 2
N_LAYERS_B = DEPTH // 2
CHUNK = 64
EPS = 1e-6

GDN_HEADS = 8
GDN_DK = 128
GDN_DV = 128
GDN_CONV = 4
GDN_QK = GDN_HEADS * GDN_DK
GDN_V = GDN_HEADS * GDN_DV
GDN_QKV = 2 * GDN_QK + GDN_V
GDN_IN = 2 * GDN_QK + 2 * GDN_V + 2 * GDN_HEADS

GLA_HEADS = 4
GLA_DK = D_MODEL // 2 // GLA_HEADS
GLA_DV = D_MODEL // GLA_HEADS
GLA_GATE_RANK = 16
GLA_TAU = 16.0
GLA_K = GLA_HEADS * GLA_DK
GLA_V = GLA_HEADS * GLA_DV
GLA_IN = 2 * GLA_K + 2 * GLA_V + GLA_GATE_RANK

D_FF = 2816
FFN_CONV = 3

kernel_name = "hybrid_gdn_gla_convglu_sandwich"


def rms_norm(x, g):
    xf = x.astype(jnp.float32)
    y = xf * lax.rsqrt(jnp.mean(xf * xf, axis=-1, keepdims=True) + EPS)
    return (y * g.astype(jnp.float32)).astype(x.dtype)


def l2_norm(t):
    return t * lax.rsqrt(jnp.sum(t * t, axis=-1, keepdims=True) + EPS)


def causal_depthwise_conv(x, w):
    k_width, channels = w.shape
    return lax.conv_general_dilated(
        x, w[:, None, :].astype(x.dtype), window_strides=(1,),
        padding=[(k_width - 1, 0)], dimension_numbers=('NWC', 'WIO', 'NWC'),
        feature_group_count=channels)


def heads_to_chunks(t, n_heads):
    b, s, _ = t.shape
    return t.reshape(b, s // CHUNK, CHUNK, n_heads, -1).transpose(0, 3, 1, 2, 4)


def gates_to_chunks(t):
    b, s, h = t.shape
    return t.reshape(b, s // CHUNK, CHUNK, h).transpose(0, 3, 1, 2)


def chunks_to_heads(t):
    b, h, n, c, d = t.shape
    return t.transpose(0, 2, 3, 1, 4).reshape(b, n * c, h, d)


def gated_delta_rule_chunked(q, k, v, g, beta):
    c = q.shape[-2]
    dv = v.shape[-1]
    idx = jnp.arange(c)
    causal = idx[:, None] >= idx[None, :]
    strict = idx[:, None] > idx[None, :]
    gc = jnp.cumsum(g, axis=-1)
    decay = jnp.exp(jnp.where(causal, gc[..., :, None] - gc[..., None, :], -jnp.inf))
    kb = k * beta[..., None]
    a = jnp.where(strict, jnp.einsum('bhnid,bhnjd->bhnij', kb, k) * decay, 0.0)
    eye = jnp.eye(c, dtype=q.dtype)
    rhs = jnp.concatenate([v * beta[..., None], kb * jnp.exp(gc)[..., None]], axis=-1)
    sol = lax.linalg.triangular_solve(a + eye, rhs, left_side=True, lower=True)
    u, w = sol[..., :dv], sol[..., dv:]
    qk = jnp.where(causal, jnp.einsum('bhnid,bhnjd->bhnij', q, k) * decay, 0.0)
    q_dec = q * jnp.exp(gc)[..., None]
    k_dec = k * jnp.exp(gc[..., -1:] - gc)[..., None]
    chunk_decay = jnp.exp(gc[..., -1])

    def step(state, inp):
        qk_c, qd_c, w_c, u_c, kd_c, cd_c = inp
        v_new = u_c - jnp.einsum('bhcd,bhde->bhce', w_c, state)
        o = jnp.einsum('bhcd,bhde->bhce', qd_c, state) + jnp.einsum('bhij,bhje->bhie', qk_c, v_new)
        state = state * cd_c[..., None, None] + jnp.einsum('bhcd,bhce->bhde', kd_c, v_new)
        return state, o

    xs = tuple(jnp.moveaxis(t, 2, 0) for t in (qk, q_dec, w, u, k_dec, chunk_decay))
    b, h = q.shape[0], q.shape[1]
    s0 = jnp.zeros((b, h, q.shape[-1], dv), q.dtype)
    _, o = lax.scan(step, s0, xs)
    return jnp.moveaxis(o, 0, 2)


def gla_chunked(q, k, v, log_a):
    c = q.shape[-2]
    idx = jnp.arange(c)
    causal = (idx[:, None] >= idx[None, :])[..., None]
    bc = jnp.cumsum(log_a, axis=-2)
    q_dec = q * jnp.exp(bc)
    k_dec = k * jnp.exp(bc[..., -1:, :] - bc)
    chunk_decay = jnp.exp(bc[..., -1, :])

    def step(state, inp):
        q_c, k_c, v_c, b_c, qd_c, kd_c, cd_c = inp
        rel = jnp.where(causal, b_c[:, :, :, None, :] - b_c[:, :, None, :, :], -jnp.inf)
        attn = jnp.sum(q_c[:, :, :, None, :] * k_c[:, :, None, :, :] * jnp.exp(rel), axis=-1)
        o = jnp.einsum('bhcd,bhde->bhce', qd_c, state) + jnp.einsum('bhij,bhje->bhie', attn, v_c)
        state = state * cd_c[..., :, None] + jnp.einsum('bhcd,bhce->bhde', kd_c, v_c)
        return state, o

    xs = tuple(jnp.moveaxis(t, 2, 0) for t in (q, k, v, bc, q_dec, k_dec, chunk_decay))
    b, h = q.shape[0], q.shape[1]
    s0 = jnp.zeros((b, h, q.shape[-1], v.shape[-1]), q.dtype)
    _, o = lax.scan(step, s0, xs)
    return jnp.moveaxis(o, 0, 2)


def gated_deltanet_mixer(h, w_in, conv_w, a_log, dt_bias, o_gain, w_out):
    b, s, _ = h.shape
    p = h @ w_in
    qkv = jax.nn.silu(causal_depthwise_conv(p[..., :GDN_QKV], conv_w))
    q, k, v = jnp.split(qkv.astype(jnp.float32), [GDN_QK, 2 * GDN_QK], axis=-1)
    z, b_logit, a_logit = jnp.split(p[..., GDN_QKV:], [GDN_V, GDN_V + GDN_HEADS], axis=-1)
    q = l2_norm(heads_to_chunks(q, GDN_HEADS)) * (GDN_DK ** -0.5)
    k = l2_norm(heads_to_chunks(k, GDN_HEADS))
    v = heads_to_chunks(v, GDN_HEADS)
    beta = gates_to_chunks(jax.nn.sigmoid(b_logit.astype(jnp.float32)))
    g = -jnp.exp(a_log.astype(jnp.float32)) * jax.nn.softplus(
        a_logit.astype(jnp.float32) + dt_bias.astype(jnp.float32))
    o = chunks_to_heads(gated_delta_rule_chunked(q, k, v, gates_to_chunks(g), beta))
    o = rms_norm(o, o_gain) * jax.nn.silu(z.astype(jnp.float32).reshape(b, s, GDN_HEADS, GDN_DV))
    return o.reshape(b, s, GDN_V).astype(h.dtype) @ w_out


def gla_mixer(h, w_in, w_gate_up, gate_bias, o_gain, w_out):
    b, s, _ = h.shape
    p = h @ w_in
    q, k, v, r, gate_low = jnp.split(
        p, [GLA_K, 2 * GLA_K, 2 * GLA_K + GLA_V, 2 * GLA_K + 2 * GLA_V], axis=-1)
    log_a = jax.nn.log_sigmoid((gate_low @ w_gate_up + gate_bias).astype(jnp.float32)) / GLA_TAU
    q = heads_to_chunks(q.astype(jnp.float32), GLA_HEADS) * (GLA_DK ** -0.5)
    k = heads_to_chunks(k.astype(jnp.float32), GLA_HEADS)
    v = heads_to_chunks(v.astype(jnp.float32), GLA_HEADS)
    o = chunks_to_heads(gla_chunked(q, k, v, heads_to_chunks(log_a, GLA_HEADS)))
    o = rms_norm(o, o_gain) * jax.nn.silu(r.astype(jnp.float32).reshape(b, s, GLA_HEADS, GLA_DV))
    return o.reshape(b, s, GLA_V).astype(h.dtype) @ w_out


def conv_glu_ffn(h, w_up, conv_w, w_down):
    u = causal_depthwise_conv(h @ w_up, conv_w)
    gate, val = jnp.split(u, 2, axis=-1)
    return (jax.nn.silu(gate) * val) @ w_down


def setup_inputs(seed: int = 0) -> dict:
    key = jax.random.key(seed)
    ks = jax.random.split(key, 20)
    f32 = jnp.float32
    nrm = lambda k, shape, scale: jax.random.normal(k, shape, f32) * scale
    gain = lambda k, shape: 1.0 + 0.05 * jax.random.normal(k, shape, f32)
    dt = jnp.exp(jax.random.uniform(ks[4], (N_LAYERS_A, GDN_HEADS), f32,
                                    np.log(1e-3), np.log(1e-1)))
    return {
        "x": jax.random.normal(ks[0], (BATCH, SEQ, D_MODEL), f32),
        "gdn_w_in": nrm(ks[1], (N_LAYERS_A, D_MODEL, GDN_IN), D_MODEL ** -0.5),
        "gdn_conv_w": nrm(ks[2], (N_LAYERS_A, GDN_CONV, GDN_QKV), GDN_CONV ** -0.5),
        "gdn_a_log": jnp.log(jax.random.uniform(ks[3], (N_LAYERS_A, GDN_HEADS), f32, 1.0, 16.0)),
        "gdn_dt_bias": dt + jnp.log(-jnp.expm1(-dt)),
        "gdn_o_norm": gain(ks[5], (N_LAYERS_A, GDN_DV)),
        "gdn_w_out": nrm(ks[6], (N_LAYERS_A, GDN_V, D_MODEL), GDN_V ** -0.5),
        "gla_w_in": nrm(ks[7], (N_LAYERS_B, D_MODEL, GLA_IN), D_MODEL ** -0.5),
        "gla_w_gate_up": nrm(ks[8], (N_LAYERS_B, GLA_GATE_RANK, GLA_K), GLA_GATE_RANK ** -0.5),
        "gla_gate_bias": nrm(ks[9], (N_LAYERS_B, GLA_K), 0.1),
        "gla_o_norm": gain(ks[10], (N_LAYERS_B, GLA_DV)),
        "gla_w_out": nrm(ks[11], (N_LAYERS_B, GLA_V, D_MODEL), GLA_V ** -0.5),
        "mix_pre_norm": gain(ks[12], (DEPTH, D_MODEL)),
        "mix_post_norm": gain(ks[13], (DEPTH, D_MODEL)),
        "ffn_pre_norm": gain(ks[14], (DEPTH, D_MODEL)),
        "ffn_post_norm": gain(ks[15], (DEPTH, D_MODEL)),
        "ffn_w_up": nrm(ks[16], (DEPTH, D_MODEL, 2 * D_FF), D_MODEL ** -0.5),
        "ffn_conv_w": nrm(ks[17], (DEPTH, FFN_CONV, 2 * D_FF), FFN_CONV ** -0.5),
        "ffn_w_down": nrm(ks[18], (DEPTH, D_FF, D_MODEL), D_FF ** -0.5),
    }


def reference(x, gdn_w_in, gdn_conv_w, gdn_a_log, gdn_dt_bias, gdn_o_norm, gdn_w_out,
              gla_w_in, gla_w_gate_up, gla_gate_bias, gla_o_norm, gla_w_out,
              mix_pre_norm, mix_post_norm, ffn_pre_norm, ffn_post_norm,
              ffn_w_up, ffn_conv_w, ffn_w_down):
    for layer in range(DEPTH):
        j = layer // N_MIXERS
        hn = rms_norm(x, mix_pre_norm[layer])
        if layer % N_MIXERS == 0:
            m = gated_deltanet_mixer(hn, gdn_w_in[j], gdn_conv_w[j], gdn_a_log[j],
                                     gdn_dt_bias[j], gdn_o_norm[j], gdn_w_out[j])
        else:
            m = gla_mixer(hn, gla_w_in[j], gla_w_gate_up[j], gla_gate_bias[j],
                          gla_o_norm[j], gla_w_out[j])
        x = x + rms_norm(m, mix_post_norm[layer])
        f = conv_glu_ffn(rms_norm(x, ffn_pre_norm[layer]), ffn_w_up[layer],
                         ffn_conv_w[layer], ffn_w_down[layer])
        x = x + rms_norm(f, ffn_post_norm[layer])
    return x
```

```python
import functools

import jax
import jax.numpy as jnp
from jax import lax
from jax.experimental import pallas as pl
from jax.experimental.pallas import tpu as pltpu

EPS = 1e-6
CHUNK = 64
LANES = 128
SUBLANES = 8
D_MODEL = 1024

GDN_HEADS = 8
GDN_DK = 128
GDN_DV = 128
GDN_CONV = 4
GDN_QK = GDN_HEADS * GDN_DK
GDN_V = GDN_HEADS * GDN_DV
GDN_QKV = 2 * GDN_QK + GDN_V
GDN_MAIN = GDN_QKV + GDN_V
GDN_IN_PAD = GDN_MAIN + LANES

GLA_HEADS = 4
GLA_DK = 128
GLA_DV = 256
GLA_GATE_RANK = 16
GLA_TAU = 16.0
GLA_K = GLA_HEADS * GLA_DK
GLA_V = GLA_HEADS * GLA_DV
GLA_MAIN = 2 * GLA_K + 2 * GLA_V
GLA_IN_PAD = GLA_MAIN + LANES
GLA_SUB = 16

D_FF = 2816
FFN_CONV = 3
FFN_TILE = 256

VMEM_LIMIT = 56 * 1024 * 1024

BF16 = jnp.bfloat16
F32 = jnp.float32


def _dot(a, b):
    return jnp.dot(a.astype(BF16), b.astype(BF16), preferred_element_type=F32)


def _dot_nt(a, b):
    return lax.dot_general(a.astype(BF16), b.astype(BF16), (((1,), (1,)), ((), ())),
                           preferred_element_type=F32)


def _dot_tn(a, b):
    return lax.dot_general(a.astype(BF16), b.astype(BF16), (((0,), (0,)), ((), ())),
                           preferred_element_type=F32)


def _rms(x, g):
    return x * lax.rsqrt(jnp.mean(x * x, axis=-1, keepdims=True) + EPS) * g


def _silu(x):
    return x * jax.nn.sigmoid(x)


def _softplus(x):
    return jnp.maximum(x, 0.0) + jnp.log1p(jnp.exp(-jnp.abs(x)))


def _chunk_cumsum_matrix(n):
    i = lax.broadcasted_iota(jnp.int32, (n, n), 0)
    j = lax.broadcasted_iota(jnp.int32, (n, n), 1)
    return jnp.where((j <= i) & ((i // CHUNK) == (j // CHUNK)), 1.0, 0.0).astype(F32)


def _params(*sem):
    return pltpu.CompilerParams(dimension_semantics=sem, vmem_limit_bytes=VMEM_LIMIT)


def _norm_matmul_kernel(x_ref, g_ref, w_ref, o_ref):
    h = _rms(x_ref[...], g_ref[...])
    o_ref[...] = _dot(h, w_ref[...])


def _norm_matmul(x2, g, w, tm=256):
    t, d = x2.shape
    n = w.shape[1]
    return pl.pallas_call(
        _norm_matmul_kernel,
        grid=(t // tm,),
        in_specs=[pl.BlockSpec((tm, d), lambda i: (i, 0)),
                  pl.BlockSpec((1, d), lambda i: (0, 0)),
                  pl.BlockSpec((d, n), lambda i: (0, 0))],
        out_specs=pl.BlockSpec((tm, n), lambda i: (i, 0)),
        out_shape=jax.ShapeDtypeStruct((t, n), F32),
        compiler_params=_params("arbitrary"),
        name="norm_matmul",
    )(x2, g, w)


def _matmul_norm_res_kernel(o_ref, w_ref, g_ref, x_ref, y_ref):
    m = jnp.dot(o_ref[...], w_ref[...], preferred_element_type=F32)
    y_ref[...] = x_ref[...] + _rms(m, g_ref[...])


def _matmul_norm_res(o2, w, g, x2, tm=512):
    t, k = o2.shape
    d = w.shape[1]
    return pl.pallas_call(
        _matmul_norm_res_kernel,
        grid=(t // tm,),
        in_specs=[pl.BlockSpec((tm, k), lambda i: (i, 0)),
                  pl.BlockSpec((k, d), lambda i: (0, 0)),
                  pl.BlockSpec((1, d), lambda i: (0, 0)),
                  pl.BlockSpec((tm, d), lambda i: (i, 0))],
        out_specs=pl.BlockSpec((tm, d), lambda i: (i, 0)),
        out_shape=jax.ShapeDtypeStruct((t, d), F32),
        compiler_params=_params("arbitrary"),
        name="matmul_norm_res",
    )(o2, w, g, x2)


def _gdn_kernel(p_ref, convw_ref, alog_ref, dtb_ref, ogain_ref, o_ref,
                xbuf, halo, q_s, k_s, v_s, beta_s, gc_s, gct_s, state, *, tc):
    nchunks = tc // CHUNK

    @pl.when(pl.program_id(1) == 0)
    def _():
        halo[...] = jnp.zeros_like(halo)
        state[...] = jnp.zeros_like(state)

    xbuf[0:SUBLANES, :] = halo[...]
    xbuf[SUBLANES:SUBLANES + tc, :] = p_ref[:, 0:GDN_QKV]
    halo[...] = p_ref[tc - SUBLANES:tc, 0:GDN_QKV]

    def conv_silu(col):
        acc = convw_ref[GDN_CONV - 1:GDN_CONV, col:col + LANES] * xbuf[SUBLANES:SUBLANES + tc, col:col + LANES]
        for kk in range(GDN_CONV - 1):
            r0 = SUBLANES - (GDN_CONV - 1) + kk
            acc = acc + convw_ref[kk:kk + 1, col:col + LANES] * xbuf[r0:r0 + tc, col:col + LANES]
        return _silu(acc)

    def l2n(t):
        return t * lax.rsqrt(jnp.sum(t * t, axis=-1, keepdims=True) + EPS)

    for h in range(GDN_HEADS):
        c0 = h * GDN_DK
        q_s[:, c0:c0 + GDN_DK] = l2n(conv_silu(c0)) * (GDN_DK ** -0.5)
        k_s[:, c0:c0 + GDN_DK] = l2n(conv_silu(GDN_QK + c0))
        v_s[:, c0:c0 + GDN_DV] = conv_silu(2 * GDN_QK + h * GDN_DV)

    pg = p_ref[:, GDN_MAIN:GDN_MAIN + LANES]
    lane = lax.broadcasted_iota(jnp.int32, (tc, LANES), 1)
    beta_s[...] = jax.nn.sigmoid(pg)
    g = -jnp.exp(alog_ref[...]) * _softplus(pg + dtb_ref[...])
    g = jnp.where((lane >= GDN_HEADS) & (lane < 2 * GDN_HEADS), g, 0.0)
    gc = jnp.dot(_chunk_cumsum_matrix(tc), g, preferred_element_type=F32,
                 precision=lax.Precision.HIGHEST)
    gc_s[...] = gc
    gct = gc.T
    for c in range(nchunks):
        gct_s[c] = gct[:, c * CHUNK:(c + 1) * CHUNK]

    ii = lax.broadcasted_iota(jnp.int32, (CHUNK, CHUNK), 0)
    jj = lax.broadcasted_iota(jnp.int32, (CHUNK, CHUNK), 1)
    causal = ii >= jj
    strict = ii > jj
    eye = jnp.where(ii == jj, 1.0, 0.0).astype(F32)

    def chunk_body(c, carry):
        r0 = pl.multiple_of(c * CHUNK, CHUNK)
        rows = pl.ds(r0, CHUNK)
        for h in range(GDN_HEADS):
            hs = slice(h * GDN_DK, (h + 1) * GDN_DK)
            q = q_s[rows, hs]
            k = k_s[rows, hs]
            v = v_s[rows, hs]
            beta = beta_s[rows, h:h + 1]
            gcol = gc_s[rows, GDN_HEADS + h:GDN_HEADS + h + 1]
            grow = gct_s[c, GDN_HEADS + h:GDN_HEADS + h + 1, :]
            glast = gc_s[pl.ds(r0 + CHUNK - 1, 1), GDN_HEADS + h:GDN_HEADS + h + 1]
            decay = jnp.exp(jnp.where(causal, gcol - grow, -jnp.inf))
            kb = k * beta
            a = jnp.where(strict, _dot_nt(kb, k) * decay, 0.0)
            tinv = eye - a
            apow = a
            for _ in range(5):
                apow = _dot(apow, apow)
                tinv = tinv + _dot(tinv, apow)
            egc = jnp.exp(gcol)
            u = _dot(tinv, v * beta)
            w = _dot(tinv, kb * egc)
            qk = jnp.where(causal, _dot_nt(q, k) * decay, 0.0)
            q_dec = q * egc
            k_dec = k * jnp.exp(glast - gcol)
            s = state[h]
            v_new = u - _dot(w, s)
            o = _dot(q_dec, s) + _dot(qk, v_new)
            state[h] = s * jnp.exp(glast) + _dot_tn(k_dec, v_new)
            z = p_ref[rows, GDN_QKV + h * GDN_DV:GDN_QKV + (h + 1) * GDN_DV]
            o_ref[rows, hs] = (_rms(o, ogain_ref[...]) * _silu(z)).astype(o_ref.dtype)
        return carry

    lax.fori_loop(0, nchunks, chunk_body, 0)


def _gdn_core(p, conv_w, a_log, dt_bias, o_gain, batch, seq, tc=256):
    t = batch * seq
    nb = seq // tc
    pad = jnp.zeros((LANES - 2 * GDN_HEADS,), F32)
    alog_row = jnp.concatenate([jnp.zeros((GDN_HEADS,), F32), a_log, pad])[None, :]
    dtb_row = jnp.concatenate([jnp.zeros((GDN_HEADS,), F32), dt_bias, pad])[None, :]
    return pl.pallas_call(
        functools.partial(_gdn_kernel, tc=tc),
        grid=(batch, nb),
        in_specs=[pl.BlockSpec((tc, GDN_IN_PAD), lambda b, i: (b * nb + i, 0)),
                  pl.BlockSpec((GDN_CONV, GDN_QKV), lambda b, i: (0, 0)),
                  pl.BlockSpec((1, LANES), lambda b, i: (0, 0)),
                  pl.BlockSpec((1, LANES), lambda b, i: (0, 0)),
                  pl.BlockSpec((1, GDN_DV), lambda b, i: (0, 0))],
        out_specs=pl.BlockSpec((tc, GDN_V), lambda b, i: (b * nb + i, 0)),
        out_shape=jax.ShapeDtypeStruct((t, GDN_V), BF16),
        scratch_shapes=[pltpu.VMEM((tc + SUBLANES, GDN_QKV), F32),
                        pltpu.VMEM((SUBLANES, GDN_QKV), F32),
                        pltpu.VMEM((tc, GDN_QK), F32),
                        pltpu.VMEM((tc, GDN_QK), F32),
                        pltpu.VMEM((tc, GDN_V), F32),
                        pltpu.VMEM((tc, LANES), F32),
                        pltpu.VMEM((tc, LANES), F32),
                        pltpu.VMEM((tc // CHUNK, LANES, CHUNK), F32),
                        pltpu.VMEM((GDN_HEADS, GDN_DK, GDN_DV), F32)],
        compiler_params=_params("arbitrary", "arbitrary"),
        name="gdn_core",
    )(p, conv_w, alog_row, dtb_row, o_gain[None, :])


def _gla_kernel(p_ref, wgu_ref, gbias_ref, ogain_ref, o_ref, bc_s, state, *, tc):
    nchunks = tc // CHUNK

    @pl.when(pl.program_id(1) == 0)
    def _():
        state[...] = jnp.zeros_like(state)

    z = _dot(p_ref[:, GLA_MAIN:GLA_MAIN + LANES], wgu_ref[...]) + gbias_ref[...]
    log_a = -_softplus(-z) * (1.0 / GLA_TAU)
    bc_s[...] = jnp.dot(_chunk_cumsum_matrix(tc), log_a, preferred_element_type=F32,
                        precision=lax.Precision.HIGHEST)

    isub = lax.broadcasted_iota(jnp.int32, (GLA_SUB, CHUNK), 0)
    jsub = lax.broadcasted_iota(jnp.int32, (GLA_SUB, CHUNK), 1)
    irow = lax.broadcasted_iota(jnp.int32, (GLA_SUB, GLA_DK), 0)
    jrow = lax.broadcasted_iota(jnp.int32, (CHUNK, GLA_DK), 0)
    scale = GLA_DK ** -0.5

    def chunk_body(c, carry):
        r0 = pl.multiple_of(c * CHUNK, CHUNK)
        rows = pl.ds(r0, CHUNK)
        for h in range(GLA_HEADS):
            ks = slice(h * GLA_DK, (h + 1) * GLA_DK)
            vs = slice(h * GLA_DV, (h + 1) * GLA_DV)
            q = p_ref[rows, h * GLA_DK:(h + 1) * GLA_DK] * scale
            k = p_ref[rows, GLA_K + h * GLA_DK:GLA_K + (h + 1) * GLA_DK]
            v = p_ref[rows, 2 * GLA_K + h * GLA_DV:2 * GLA_K + (h + 1) * GLA_DV]
            b = bc_s[rows, ks]
            blast = b[CHUNK - 1:CHUNK, :]
            q_dec = q * jnp.exp(b)
            k_dec = k * jnp.exp(blast - b)
            blocks = []
            for blk in range(CHUNK // GLA_SUB):
                i0 = blk * GLA_SUB
                q_i = q[i0:i0 + GLA_SUB]
                b_i = b[i0:i0 + GLA_SUB]
                k_i = k[i0:i0 + GLA_SUB]
                acc = jnp.zeros((GLA_SUB, CHUNK), F32)
                if blk > 0:
                    ref_row = b[i0:i0 + 1]
                    q_t = q_i * jnp.exp(b_i - ref_row)
                    k_t = k * jnp.exp(jnp.where(jrow < i0, ref_row - b, -jnp.inf))
                    acc = _dot_nt(q_t, k_t)
                for j in range(GLA_SUB):
                    e = jnp.exp(jnp.where(irow >= j, b_i - b_i[j:j + 1], -jnp.inf))
                    col = jnp.sum(q_i * k_i[j:j + 1] * e, axis=-1, keepdims=True)
                    acc = jnp.where(jsub == i0 + j, col, acc)
                blocks.append(acc)
            attn = jnp.concatenate(blocks, axis=0)
            st = state[h]
            o = _dot_nt(q_dec, st) + _dot(attn, v)
            state[h] = st * jnp.exp(blast) + _dot_tn(v, k_dec)
            r = p_ref[rows, 2 * GLA_K + GLA_V + h * GLA_DV:2 * GLA_K + GLA_V + (h + 1) * GLA_DV]
            o_ref[rows, vs] = (_rms(o, ogain_ref[...]) * _silu(r)).astype(o_ref.dtype)
        return carry

    lax.fori_loop(0, nchunks, chunk_body, 0)


def _gla_core(p, w_gate_up, gate_bias, o_gain, batch, seq, tc=256):
    t = batch * seq
    nb = seq // tc
    wgu = jnp.zeros((LANES, GLA_K), BF16).at[:GLA_GATE_RANK].set(w_gate_up.astype(BF16))
    return pl.pallas_call(
        functools.partial(_gla_kernel, tc=tc),
        grid=(batch, nb),
        in_specs=[pl.BlockSpec((tc, GLA_IN_PAD), lambda b, i: (b * nb + i, 0)),
                  pl.BlockSpec((LANES, GLA_K), lambda b, i: (0, 0)),
                  pl.BlockSpec((1, GLA_K), lambda b, i: (0, 0)),
                  pl.BlockSpec((1, GLA_DV), lambda b, i: (0, 0))],
        out_specs=pl.BlockSpec((tc, GLA_V), lambda b, i: (b * nb + i, 0)),
        out_shape=jax.ShapeDtypeStruct((t, GLA_V), BF16),
        scratch_shapes=[pltpu.VMEM((tc, GLA_K), F32),
                        pltpu.VMEM((GLA_HEADS, GLA_DV, GLA_DK), F32)],
        compiler_params=_params("arbitrary", "arbitrary"),
        name="gla_core",
    )(p, wgu, gate_bias[None, :], o_gain[None, :])


def _ffn_kernel(x_ref, gpre_ref, wup_ref, convw_ref, wdown_ref, gpost_ref, y_ref,
                h_s, ubuf, halo, act_s, *, tm):
    @pl.when(pl.program_id(1) == 0)
    def _():
        halo[...] = jnp.zeros_like(halo)

    x = x_ref[...]
    h_s[...] = _rms(x, gpre_ref[...]).astype(BF16)

    def conv_tile(col):
        u = jnp.dot(h_s[...], wup_ref[:, col:col + FFN_TILE], preferred_element_type=F32)
        ubuf[0:SUBLANES, :] = halo[:, col:col + FFN_TILE]
        ubuf[SUBLANES:SUBLANES + tm, :] = u
        halo[:, col:col + FFN_TILE] = u[tm - SUBLANES:tm]
        acc = convw_ref[FFN_CONV - 1:FFN_CONV, col:col + FFN_TILE] * u
        for kk in range(FFN_CONV - 1):
            r0 = SUBLANES - (FFN_CONV - 1) + kk
            acc = acc + convw_ref[kk:kk + 1, col:col + FFN_TILE] * ubuf[r0:r0 + tm, :]
        return acc

    for j in range(D_FF // FFN_TILE):
        gate = conv_tile(j * FFN_TILE)
        val = conv_tile(D_FF + j * FFN_TILE)
        act_s[:, j * FFN_TILE:(j + 1) * FFN_TILE] = (_silu(gate) * val).astype(BF16)

    f = jnp.dot(act_s[...], wdown_ref[...], preferred_element_type=F32)
    y_ref[...] = x + _rms(f, gpost_ref[...])


def _ffn(x2, g_pre, w_up, conv_w, w_down, g_post, batch, seq, tm=512):
    t, d = x2.shape
    nb = seq // tm
    const = lambda b, i: (0, 0)
    return pl.pallas_call(
        functools.partial(_ffn_kernel, tm=tm),
        grid=(batch, nb),
        in_specs=[pl.BlockSpec((tm, d), lambda b, i: (b * nb + i, 0)),
                  pl.BlockSpec((1, d), const),
                  pl.BlockSpec((d, 2 * D_FF), const, pipeline_mode=pl.Buffered(1)),
                  pl.BlockSpec((FFN_CONV, 2 * D_FF), const),
                  pl.BlockSpec((D_FF, d), const, pipeline_mode=pl.Buffered(1)),
                  pl.BlockSpec((1, d), const)],
        out_specs=pl.BlockSpec((tm, d), lambda b, i: (b * nb + i, 0)),
        out_shape=jax.ShapeDtypeStruct((t, d), F32),
        scratch_shapes=[pltpu.VMEM((tm, d), BF16),
                        pltpu.VMEM((tm + SUBLANES, FFN_TILE), F32),
                        pltpu.VMEM((SUBLANES, 2 * D_FF), F32),
                        pltpu.VMEM((tm, D_FF), BF16)],
        compiler_params=_params("arbitrary", "arbitrary"),
        name="conv_glu_ffn",
    )(x2, g_pre, w_up, conv_w, w_down, g_post)


def _pad_cols(w, n):
    return jnp.pad(w, ((0, 0), (0, n - w.shape[1])))


def kernel(x, gdn_w_in, gdn_conv_w, gdn_a_log, gdn_dt_bias, gdn_o_norm, gdn_w_out,
           gla_w_in, gla_w_gate_up, gla_gate_bias, gla_o_norm, gla_w_out,
           mix_pre_norm, mix_post_norm, ffn_pre_norm, ffn_post_norm,
           ffn_w_up, ffn_conv_w, ffn_w_down):
    batch, seq, d = x.shape
    depth = mix_pre_norm.shape[0]
    x2 = x.reshape(batch * seq, d)
    for layer in range(depth):
        j = layer // 2
        g_pre = mix_pre_norm[layer][None, :]
        g_post = mix_post_norm[layer][None, :]
        if layer % 2 == 0:
            w_in = _pad_cols(gdn_w_in[j], GDN_IN_PAD).astype(BF16)
            p = _norm_matmul(x2, g_pre, w_in)
            o = _gdn_core(p, gdn_conv_w[j], gdn_a_log[j], gdn_dt_bias[j], gdn_o_norm[j], batch, seq)
            x2 = _matmul_norm_res(o, gdn_w_out[j].astype(BF16), g_post, x2)
        else:
            w_in = _pad_cols(gla_w_in[j], GLA_IN_PAD).astype(BF16)
            p = _norm_matmul(x2, g_pre, w_in)
            o = _gla_core(p, gla_w_gate_up[j], gla_gate_bias[j], gla_o_norm[j], batch, seq)
            x2 = _matmul_norm_res(o, gla_w_out[j].astype(BF16), g_post, x2)
        x2 = _ffn(x2, ffn_pre_norm[layer][None, :], ffn_w_up[layer].astype(BF16), ffn_conv_w[layer],
                  ffn_w_down[layer].astype(BF16), ffn_post_norm[layer][None, :], batch, seq)
    return x2.reshape(batch, seq, d)
```

```python
import functools

import jax
import jax.numpy as jnp
from jax import lax
from jax.experimental import pallas as pl
from jax.experimental.pallas import tpu as pltpu

EPS = 1e-6
CHUNK = 64
LANES = 128
SUBLANES = 8
D_MODEL = 1024

GDN_HEADS = 8
GDN_DK = 128
GDN_DV = 128
GDN_CONV = 4
GDN_QK = GDN_HEADS * GDN_DK
GDN_V = GDN_HEADS * GDN_DV
GDN_QKV = 2 * GDN_QK + GDN_V
GDN_MAIN = GDN_QKV + GDN_V
GDN_IN_PAD = GDN_MAIN + LANES

GLA_HEADS = 4
GLA_DK = 128
GLA_DV = 256
GLA_GATE_RANK = 16
GLA_TAU = 16.0
GLA_K = GLA_HEADS * GLA_DK
GLA_V = GLA_HEADS * GLA_DV
GLA_MAIN = 2 * GLA_K + 2 * GLA_V
GLA_IN_PAD = GLA_MAIN + LANES
GLA_SUB = 16

D_FF = 2816
FFN_CONV = 3
FFN_TILE = 256

VMEM_LIMIT = 56 * 1024 * 1024

BF16 = jnp.bfloat16
F32 = jnp.float32


def _dot(a, b):
    return jnp.dot(a.astype(BF16), b.astype(BF16), preferred_element_type=F32)


def _dot_nt(a, b):
    return lax.dot_general(a.astype(BF16), b.astype(BF16), (((1,), (1,)), ((), ())),
                           preferred_element_type=F32)


def _dot_tn(a, b):
    return lax.dot_general(a.astype(BF16), b.astype(BF16), (((0,), (0,)), ((), ())),
                           preferred_element_type=F32)


def _rms(x, g):
    return x * lax.rsqrt(jnp.mean(x * x, axis=-1, keepdims=True) + EPS) * g


def _silu(x):
    return x * jax.nn.sigmoid(x)


def _softplus(x):
    return jnp.maximum(x, 0.0) + jnp.log1p(jnp.exp(-jnp.abs(x)))


def _chunk_cumsum_matrix(n):
    i = lax.broadcasted_iota(jnp.int32, (n, n), 0)
    j = lax.broadcasted_iota(jnp.int32, (n, n), 1)
    return jnp.where((j <= i) & ((i // CHUNK) == (j // CHUNK)), 1.0, 0.0).astype(F32)


def _params(*sem):
    return pltpu.CompilerParams(dimension_semantics=sem, vmem_limit_bytes=VMEM_LIMIT)


def _norm_matmul_kernel(x_ref, g_ref, w_ref, o_ref):
    h = _rms(x_ref[...], g_ref[...])
    o_ref[...] = _dot(h, w_ref[...])


def _norm_matmul(x2, g, w, tm=256):
    t, d = x2.shape
    n = w.shape[1]
    return pl.pallas_call(
        _norm_matmul_kernel,
        grid=(t // tm,),
        in_specs=[pl.BlockSpec((tm, d), lambda i: (i, 0)),
                  pl.BlockSpec((1, d), lambda i: (0, 0)),
                  pl.BlockSpec((d, n), lambda i: (0, 0))],
        out_specs=pl.BlockSpec((tm, n), lambda i: (i, 0)),
        out_shape=jax.ShapeDtypeStruct((t, n), F32),
        compiler_params=_params("arbitrary"),
        name="norm_matmul",
    )(x2, g, w)


def _matmul_norm_res_kernel(o_ref, w_ref, g_ref, x_ref, y_ref):
    m = jnp.dot(o_ref[...], w_ref[...], preferred_element_type=F32)
    y_ref[...] = x_ref[...] + _rms(m, g_ref[...])


def _matmul_norm_res(o2, w, g, x2, tm=512):
    t, k = o2.shape
    d = w.shape[1]
    return pl.pallas_call(
        _matmul_norm_res_kernel,
        grid=(t // tm,),
        in_specs=[pl.BlockSpec((tm, k), lambda i: (i, 0)),
                  pl.BlockSpec((k, d), lambda i: (0, 0)),
                  pl.BlockSpec((1, d), lambda i: (0, 0)),
                  pl.BlockSpec((tm, d), lambda i: (i, 0))],
        out_specs=pl.BlockSpec((tm, d), lambda i: (i, 0)),
        out_shape=jax.ShapeDtypeStruct((t, d), F32),
        compiler_params=_params("arbitrary"),
        name="matmul_norm_res",
    )(o2, w, g, x2)


def _gdn_kernel(p_ref, convw_ref, alog_ref, dtb_ref, ogain_ref, o_ref,
                xbuf, halo, q_s, k_s, v_s, beta_s, gc_s, gct_s, state, *, tc):
    nchunks = tc // CHUNK

    @pl.when(pl.program_id(1) == 0)
    def _():
        halo[...] = jnp.zeros_like(halo)
        state[...] = jnp.zeros_like(state)

    xbuf[0:SUBLANES, :] = halo[...]
    xbuf[SUBLANES:SUBLANES + tc, :] = p_ref[:, 0:GDN_QKV]
    halo[...] = p_ref[tc - SUBLANES:tc, 0:GDN_QKV]

    def conv_silu(col):
        acc = convw_ref[GDN_CONV - 1:GDN_CONV, col:col + LANES] * xbuf[SUBLANES:SUBLANES + tc, col:col + LANES]
        for kk in range(GDN_CONV - 1):
            r0 = SUBLANES - (GDN_CONV - 1) + kk
            acc = acc + convw_ref[kk:kk + 1, col:col + LANES] * xbuf[r0:r0 + tc, col:col + LANES]
        return _silu(acc)

    def l2n(t):
        return t * lax.rsqrt(jnp.sum(t * t, axis=-1, keepdims=True) + EPS)

    for h in range(GDN_HEADS):
        c0 = h * GDN_DK
        q_s[:, c0:c0 + GDN_DK] = l2n(conv_silu(c0)) * (GDN_DK ** -0.5)
        k_s[:, c0:c0 + GDN_DK] = l2n(conv_silu(GDN_QK + c0))
        v_s[:, c0:c0 + GDN_DV] = conv_silu(2 * GDN_QK + h * GDN_DV)

    pg = p_ref[:, GDN_MAIN:GDN_MAIN + LANES]
    lane = lax.broadcasted_iota(jnp.int32, (tc, LANES), 1)
    beta_s[...] = jax.nn.sigmoid(pg)
    g = -jnp.exp(alog_ref[...]) * _softplus(pg + dtb_ref[...])
    g = jnp.where((lane >= GDN_HEADS) & (lane < 2 * GDN_HEADS), g, 0.0)
    gc = jnp.dot(_chunk_cumsum_matrix(tc), g, preferred_element_type=F32,
                 precision=lax.Precision.HIGHEST)
    gc_s[...] = gc
    gct = gc.T
    for c in range(nchunks):
        gct_s[c] = gct[:, c * CHUNK:(c + 1) * CHUNK]

    ii = lax.broadcasted_iota(jnp.int32, (CHUNK, CHUNK), 0)
    jj = lax.broadcasted_iota(jnp.int32, (CHUNK, CHUNK), 1)
    causal = ii >= jj
    strict = ii > jj
    eye = jnp.where(ii == jj, 1.0, 0.0).astype(F32)

    def chunk_body(c, carry):
        r0 = pl.multiple_of(c * CHUNK, CHUNK)
        rows = pl.ds(r0, CHUNK)
        heads = range(GDN_HEADS)
        hsl = [slice(h * GDN_DK, (h + 1) * GDN_DK) for h in heads]
        k = [k_s[rows, hsl[h]] for h in heads]
        beta = [beta_s[rows, h:h + 1] for h in heads]
        gcol = [gc_s[rows, GDN_HEADS + h:GDN_HEADS + h + 1] for h in heads]
        glast = [gc_s[pl.ds(r0 + CHUNK - 1, 1), GDN_HEADS + h:GDN_HEADS + h + 1] for h in heads]
        decay = [jnp.exp(jnp.where(causal, gcol[h] - gct_s[c, GDN_HEADS + h:GDN_HEADS + h + 1, :], -jnp.inf))
                 for h in heads]
        kb = [k[h] * beta[h] for h in heads]
        a = [jnp.where(strict, _dot_nt(kb[h], k[h]) * decay[h], 0.0) for h in heads]
        qk = [jnp.where(causal, _dot_nt(q_s[rows, hsl[h]], k[h]) * decay[h], 0.0) for h in heads]
        tinv = [eye - a[h] for h in heads]
        apow = a
        for _ in range(5):
            apow = [_dot(apow[h], apow[h]) for h in heads]
            tinv = [tinv[h] + _dot(tinv[h], apow[h]) for h in heads]
        egc = [jnp.exp(gcol[h]) for h in heads]
        rhs = [jnp.concatenate([v_s[rows, hsl[h]] * beta[h], kb[h] * egc[h]], axis=-1) for h in heads]
        sol = [_dot(tinv[h], rhs[h]) for h in heads]
        s = [state[h] for h in heads]
        v_new = [sol[h][:, :GDN_DV] - _dot(sol[h][:, GDN_DV:], s[h]) for h in heads]
        for h in heads:
            k_dec = k[h] * jnp.exp(glast[h] - gcol[h])
            state[h] = s[h] * jnp.exp(glast[h]) + _dot_tn(k_dec, v_new[h])
        for h in heads:
            o = _dot(q_s[rows, hsl[h]] * egc[h], s[h]) + _dot(qk[h], v_new[h])
            z = p_ref[rows, GDN_QKV + h * GDN_DV:GDN_QKV + (h + 1) * GDN_DV]
            o_ref[rows, hsl[h]] = (_rms(o, ogain_ref[...]) * _silu(z)).astype(o_ref.dtype)
        return carry

    lax.fori_loop(0, nchunks, chunk_body, 0)


def _gdn_core(p, conv_w, a_log, dt_bias, o_gain, batch, seq, tc=256):
    t = batch * seq
    nb = seq // tc
    pad = jnp.zeros((LANES - 2 * GDN_HEADS,), F32)
    alog_row = jnp.concatenate([jnp.zeros((GDN_HEADS,), F32), a_log, pad])[None, :]
    dtb_row = jnp.concatenate([jnp.zeros((GDN_HEADS,), F32), dt_bias, pad])[None, :]
    return pl.pallas_call(
        functools.partial(_gdn_kernel, tc=tc),
        grid=(batch, nb),
        in_specs=[pl.BlockSpec((tc, GDN_IN_PAD), lambda b, i: (b * nb + i, 0)),
                  pl.BlockSpec((GDN_CONV, GDN_QKV), lambda b, i: (0, 0)),
                  pl.BlockSpec((1, LANES), lambda b, i: (0, 0)),
                  pl.BlockSpec((1, LANES), lambda b, i: (0, 0)),
                  pl.BlockSpec((1, GDN_DV), lambda b, i: (0, 0))],
        out_specs=pl.BlockSpec((tc, GDN_V), lambda b, i: (b * nb + i, 0)),
        out_shape=jax.ShapeDtypeStruct((t, GDN_V), BF16),
        scratch_shapes=[pltpu.VMEM((tc + SUBLANES, GDN_QKV), F32),
                        pltpu.VMEM((SUBLANES, GDN_QKV), F32),
                        pltpu.VMEM((tc, GDN_QK), F32),
                        pltpu.VMEM((tc, GDN_QK), F32),
                        pltpu.VMEM((tc, GDN_V), F32),
                        pltpu.VMEM((tc, LANES), F32),
                        pltpu.VMEM((tc, LANES), F32),
                        pltpu.VMEM((tc // CHUNK, LANES, CHUNK), F32),
                        pltpu.VMEM((GDN_HEADS, GDN_DK, GDN_DV), F32)],
        compiler_params=_params("arbitrary", "arbitrary"),
        name="gdn_core",
    )(p, conv_w, alog_row, dtb_row, o_gain[None, :])


def _gla_kernel(p_ref, wgu_ref, gbias_ref, ogain_ref, o_ref, bc_s, state, *, tc):
    nchunks = tc // CHUNK

    @pl.when(pl.program_id(1) == 0)
    def _():
        state[...] = jnp.zeros_like(state)

    z = _dot(p_ref[:, GLA_MAIN:GLA_MAIN + LANES], wgu_ref[...]) + gbias_ref[...]
    log_a = -_softplus(-z) * (1.0 / GLA_TAU)
    bc_s[...] = jnp.dot(_chunk_cumsum_matrix(tc), log_a, preferred_element_type=F32,
                        precision=lax.Precision.HIGHEST)

    isub = lax.broadcasted_iota(jnp.int32, (GLA_SUB, CHUNK), 0)
    jsub = lax.broadcasted_iota(jnp.int32, (GLA_SUB, CHUNK), 1)
    irow = lax.broadcasted_iota(jnp.int32, (GLA_SUB, GLA_DK), 0)
    jrow = lax.broadcasted_iota(jnp.int32, (CHUNK, GLA_DK), 0)
    scale = GLA_DK ** -0.5

    def chunk_body(c, carry):
        r0 = pl.multiple_of(c * CHUNK, CHUNK)
        rows = pl.ds(r0, CHUNK)
        for h in range(GLA_HEADS):
            ks = slice(h * GLA_DK, (h + 1) * GLA_DK)
            vs = slice(h * GLA_DV, (h + 1) * GLA_DV)
            q = p_ref[rows, h * GLA_DK:(h + 1) * GLA_DK] * scale
            k = p_ref[rows, GLA_K + h * GLA_DK:GLA_K + (h + 1) * GLA_DK]
            v = p_ref[rows, 2 * GLA_K + h * GLA_DV:2 * GLA_K + (h + 1) * GLA_DV]
            b = bc_s[rows, ks]
            blast = b[CHUNK - 1:CHUNK, :]
            q_dec = q * jnp.exp(b)
            k_dec = k * jnp.exp(blast - b)
            blocks = []
            for blk in range(CHUNK // GLA_SUB):
                i0 = blk * GLA_SUB
                q_i = q[i0:i0 + GLA_SUB]
                b_i = b[i0:i0 + GLA_SUB]
                k_i = k[i0:i0 + GLA_SUB]
                acc = jnp.zeros((GLA_SUB, CHUNK), F32)
                if blk > 0:
                    ref_row = b[i0:i0 + 1]
                    q_t = q_i * jnp.exp(b_i - ref_row)
                    k_t = k * jnp.exp(jnp.where(jrow < i0, ref_row - b, -jnp.inf))
                    acc = _dot_nt(q_t, k_t)
                for j in range(GLA_SUB):
                    e = jnp.exp(jnp.where(irow >= j, b_i - b_i[j:j + 1], -jnp.inf))
                    col = jnp.sum(q_i * k_i[j:j + 1] * e, axis=-1, keepdims=True)
                    acc = jnp.where(jsub == i0 + j, col, acc)
                blocks.append(acc)
            attn = jnp.concatenate(blocks, axis=0)
            st = state[h]
            o = _dot_nt(q_dec, st) + _dot(attn, v)
            state[h] = st * jnp.exp(blast) + _dot_tn(v, k_dec)
            r = p_ref[rows, 2 * GLA_K + GLA_V + h * GLA_DV:2 * GLA_K + GLA_V + (h + 1) * GLA_DV]
            o_ref[rows, vs] = (_rms(o, ogain_ref[...]) * _silu(r)).astype(o_ref.dtype)
        return carry

    lax.fori_loop(0, nchunks, chunk_body, 0)


def _gla_core(p, w_gate_up, gate_bias, o_gain, batch, seq, tc=256):
    t = batch * seq
    nb = seq // tc
    wgu = jnp.zeros((LANES, GLA_K), BF16).at[:GLA_GATE_RANK].set(w_gate_up.astype(BF16))
    return pl.pallas_call(
        functools.partial(_gla_kernel, tc=tc),
        grid=(batch, nb),
        in_specs=[pl.BlockSpec((tc, GLA_IN_PAD), lambda b, i: (b * nb + i, 0)),
                  pl.BlockSpec((LANES, GLA_K), lambda b, i: (0, 0)),
                  pl.BlockSpec((1, GLA_K), lambda b, i: (0, 0)),
                  pl.BlockSpec((1, GLA_DV), lambda b, i: (0, 0))],
        out_specs=pl.BlockSpec((tc, GLA_V), lambda b, i: (b * nb + i, 0)),
        out_shape=jax.ShapeDtypeStruct((t, GLA_V), BF16),
        scratch_shapes=[pltpu.VMEM((tc, GLA_K), F32),
                        pltpu.VMEM((GLA_HEADS, GLA_DV, GLA_DK), F32)],
        compiler_params=_params("arbitrary", "arbitrary"),
        name="gla_core",
    )(p, wgu, gate_bias[None, :], o_gain[None, :])


def _ffn_kernel(x_ref, gpre_ref, wup_ref, convw_ref, wdown_ref, gpost_ref, y_ref,
                h_s, ubuf, halo, act_s, *, tm):
    @pl.when(pl.program_id(1) == 0)
    def _():
        halo[...] = jnp.zeros_like(halo)

    x = x_ref[...]
    h_s[...] = _rms(x, gpre_ref[...]).astype(BF16)

    def conv_tile(col):
        u = jnp.dot(h_s[...], wup_ref[:, col:col + FFN_TILE], preferred_element_type=F32)
        ubuf[0:SUBLANES, :] = halo[:, col:col + FFN_TILE]
        ubuf[SUBLANES:SUBLANES + tm, :] = u
        halo[:, col:col + FFN_TILE] = u[tm - SUBLANES:tm]
        acc = convw_ref[FFN_CONV - 1:FFN_CONV, col:col + FFN_TILE] * u
        for kk in range(FFN_CONV - 1):
            r0 = SUBLANES - (FFN_CONV - 1) + kk
            acc = acc + convw_ref[kk:kk + 1, col:col + FFN_TILE] * ubuf[r0:r0 + tm, :]
        return acc

    for j in range(D_FF // FFN_TILE):
        gate = conv_tile(j * FFN_TILE)
        val = conv_tile(D_FF + j * FFN_TILE)
        act_s[:, j * FFN_TILE:(j + 1) * FFN_TILE] = (_silu(gate) * val).astype(BF16)

    f = jnp.dot(act_s[...], wdown_ref[...], preferred_element_type=F32)
    y_ref[...] = x + _rms(f, gpost_ref[...])


def _ffn(x2, g_pre, w_up, conv_w, w_down, g_post, batch, seq, tm=512):
    t, d = x2.shape
    nb = seq // tm
    const = lambda b, i: (0, 0)
    return pl.pallas_call(
        functools.partial(_ffn_kernel, tm=tm),
        grid=(batch, nb),
        in_specs=[pl.BlockSpec((tm, d), lambda b, i: (b * nb + i, 0)),
                  pl.BlockSpec((1, d), const),
                  pl.BlockSpec((d, 2 * D_FF), const, pipeline_mode=pl.Buffered(1)),
                  pl.BlockSpec((FFN_CONV, 2 * D_FF), const),
                  pl.BlockSpec((D_FF, d), const, pipeline_mode=pl.Buffered(1)),
                  pl.BlockSpec((1, d), const)],
        out_specs=pl.BlockSpec((tm, d), lambda b, i: (b * nb + i, 0)),
        out_shape=jax.ShapeDtypeStruct((t, d), F32),
        scratch_shapes=[pltpu.VMEM((tm, d), BF16),
                        pltpu.VMEM((tm + SUBLANES, FFN_TILE), F32),
                        pltpu.VMEM((SUBLANES, 2 * D_FF), F32),
                        pltpu.VMEM((tm, D_FF), BF16)],
        compiler_params=_params("arbitrary", "arbitrary"),
        name="conv_glu_ffn",
    )(x2, g_pre, w_up, conv_w, w_down, g_post)


def _pad_cols(w, n):
    return jnp.pad(w, ((0, 0), (0, n - w.shape[1])))


def kernel(x, gdn_w_in, gdn_conv_w, gdn_a_log, gdn_dt_bias, gdn_o_norm, gdn_w_out,
           gla_w_in, gla_w_gate_up, gla_gate_bias, gla_o_norm, gla_w_out,
           mix_pre_norm, mix_post_norm, ffn_pre_norm, ffn_post_norm,
           ffn_w_up, ffn_conv_w, ffn_w_down):
    batch, seq, d = x.shape
    depth = mix_pre_norm.shape[0]
    x2 = x.reshape(batch * seq, d)
    for layer in range(depth):
        j = layer // 2
        g_pre = mix_pre_norm[layer][None, :]
        g_post = mix_post_norm[layer][None, :]
        if layer % 2 == 0:
            w_in = _pad_cols(gdn_w_in[j], GDN_IN_PAD).astype(BF16)
            p = _norm_matmul(x2, g_pre, w_in)
            o = _gdn_core(p, gdn_conv_w[j], gdn_a_log[j], gdn_dt_bias[j], gdn_o_norm[j], batch, seq)
            x2 = _matmul_norm_res(o, gdn_w_out[j].astype(BF16), g_post, x2)
        else:
            w_in = _pad_cols(gla_w_in[j], GLA_IN_PAD).astype(BF16)
            p = _norm_matmul(x2, g_pre, w_in)
            o = _gla_core(p, gla_w_gate_up[j], gla_gate_bias[j], gla_o_norm[j], batch, seq)
            x2 = _matmul_norm_res(o, gla_w_out[j].astype(BF16), g_post, x2)
        x2 = _ffn(x2, ffn_pre_norm[layer][None, :], ffn_w_up[layer].astype(BF16), ffn_conv_w[layer],
                  ffn_w_down[layer].astype(BF16), ffn_post_norm[layer][None, :], batch, seq)
    return x2.reshape(batch, seq, d)
```

```python
import functools

import jax
import jax.numpy as jnp
from jax import lax
from jax.experimental import pallas as pl
from jax.experimental.pallas import tpu as pltpu

EPS = 1e-6
CHUNK = 64
LANES = 128
SUBLANES = 8
MXU_DIM = 256
D_MODEL = 1024

GDN_HEADS = 8
GDN_DK = 128
GDN_DV = 128
GDN_CONV = 4
GDN_QK = GDN_HEADS * GDN_DK
GDN_V = GDN_HEADS * GDN_DV
GDN_QKV = 2 * GDN_QK + GDN_V
GDN_MAIN = GDN_QKV + GDN_V
GDN_IN_PAD = GDN_MAIN + LANES

GLA_HEADS = 4
GLA_DK = 128
GLA_DV = 256
GLA_GATE_RANK = 16
GLA_TAU = 16.0
GLA_K = GLA_HEADS * GLA_DK
GLA_V = GLA_HEADS * GLA_DV
GLA_MAIN = 2 * GLA_K + 2 * GLA_V
GLA_IN_PAD = GLA_MAIN + LANES
GLA_SUB = 8
GLA_LEVELS = (32, 16, 8)

D_FF = 2816
FFN_CONV = 3
FFN_TILE = MXU_DIM
FFN_DOWN_SPLIT = (3, 3, 3, 2)
SEQ_BLOCK = 256
GDN_SLOTS_PER_CHUNK = 10
GLA_SLOTS_PER_CHUNK = 6

VMEM_LIMIT = 56 * 1024 * 1024

BF16 = jnp.bfloat16
F32 = jnp.float32


def _dot(a, b):
    return jnp.dot(a.astype(BF16), b.astype(BF16), preferred_element_type=F32)


def _dot_nt(a, b):
    return lax.dot_general(a.astype(BF16), b.astype(BF16), (((1,), (1,)), ((), ())),
                           preferred_element_type=F32)


def _dot_tn(a, b):
    return lax.dot_general(a.astype(BF16), b.astype(BF16), (((0,), (0,)), ((), ())),
                           preferred_element_type=F32)


def _rms(x, g):
    return x * lax.rsqrt(jnp.mean(x * x, axis=-1, keepdims=True) + EPS) * g


def _silu(x):
    return x * jax.nn.sigmoid(x)


def _softplus(x):
    return jnp.maximum(x, 0.0) + jnp.log1p(jnp.exp(-jnp.abs(x)))


def _chunk_cumsum_matrix(n):
    i = lax.broadcasted_iota(jnp.int32, (n, n), 0)
    j = lax.broadcasted_iota(jnp.int32, (n, n), 1)
    return jnp.where((j <= i) & ((i // CHUNK) == (j // CHUNK)), 1.0, 0.0).astype(F32)


def _params(*sem):
    return pltpu.CompilerParams(dimension_semantics=sem, vmem_limit_bytes=VMEM_LIMIT)


def _norm_matmul_kernel(x_ref, g_ref, w_ref, o_ref):
    h = _rms(x_ref[...], g_ref[...])
    o_ref[...] = _dot(h, w_ref[...])


def _norm_matmul(x2, g, w, tm=256):
    t, d = x2.shape
    n = w.shape[1]
    return pl.pallas_call(
        _norm_matmul_kernel,
        grid=(t // tm,),
        in_specs=[pl.BlockSpec((tm, d), lambda i: (i, 0)),
                  pl.BlockSpec((1, d), lambda i: (0, 0)),
                  pl.BlockSpec((d, n), lambda i: (0, 0))],
        out_specs=pl.BlockSpec((tm, n), lambda i: (i, 0)),
        out_shape=jax.ShapeDtypeStruct((t, n), F32),
        compiler_params=_params("arbitrary"),
        name="norm_matmul",
    )(x2, g, w)


_FFN_SCRATCH = lambda tm: [
    pltpu.VMEM((2, tm, D_MODEL), BF16),
    pltpu.VMEM((tm, D_MODEL), F32),
    pltpu.VMEM((tm, D_MODEL), F32),
    pltpu.VMEM((tm, D_MODEL), BF16),
    pltpu.VMEM((2, tm + SUBLANES, FFN_TILE), F32),
    pltpu.VMEM((SUBLANES, 2 * D_FF), F32),
    pltpu.VMEM((tm, D_FF), BF16),
    pltpu.VMEM((tm, D_MODEL), F32),
]


def _ffn_steps(o_prev, x_ref, wout_ref, gmix_ref, gpre_ref, wup_ref, fconvw_ref, wdown_ref, gffn_ref,
               y_ref, m_s, xmid_s, h_s, ubuf, fhalo, act_s, f_s, tm):
    steps = []

    def out_proj(n):
        cs = slice(n * MXU_DIM, (n + 1) * MXU_DIM)
        m_s[:, cs] = jnp.dot(o_prev[...], wout_ref[:, cs], preferred_element_type=F32)

    def norms():
        xm = x_ref[...] + _rms(m_s[...], gmix_ref[...])
        xmid_s[...] = xm
        h_s[...] = _rms(xm, gpre_ref[...]).astype(BF16)

    def conv_tile(col, buf):
        u = jnp.dot(h_s[...], wup_ref[:, col:col + FFN_TILE], preferred_element_type=F32)
        ubuf[buf, 0:SUBLANES, :] = fhalo[:, col:col + FFN_TILE]
        ubuf[buf, SUBLANES:SUBLANES + tm, :] = u
        fhalo[:, col:col + FFN_TILE] = u[tm - SUBLANES:tm]
        acc = fconvw_ref[FFN_CONV - 1:FFN_CONV, col:col + FFN_TILE] * u
        for kk in range(FFN_CONV - 1):
            r0 = SUBLANES - (FFN_CONV - 1) + kk
            acc = acc + fconvw_ref[kk:kk + 1, col:col + FFN_TILE] * ubuf[buf, r0:r0 + tm, :]
        return acc

    def up(j):
        gate = conv_tile(j * FFN_TILE, 0)
        val = conv_tile(D_FF + j * FFN_TILE, 1)
        act_s[:, j * FFN_TILE:(j + 1) * FFN_TILE] = (_silu(gate) * val).astype(BF16)

    def down(k0, k1):
        part = jnp.dot(act_s[:, k0:k1], wdown_ref[k0:k1, :], preferred_element_type=F32)
        if k0 == 0:
            f_s[...] = part
        else:
            f_s[...] += part

    def finish():
        y_ref[...] = xmid_s[...] + _rms(f_s[...], gffn_ref[...])

    for n in range(D_MODEL // MXU_DIM):
        steps.append(functools.partial(out_proj, n))
    steps.append(norms)
    for j in range(D_FF // FFN_TILE):
        steps.append(functools.partial(up, j))
    k0 = 0
    for width in FFN_DOWN_SPLIT:
        steps.append(functools.partial(down, k0, k0 + width * FFN_TILE))
        k0 += width * FFN_TILE
    steps.append(finish)
    return steps


class _Interleaver:
    def __init__(self, steps, slots):
        self._steps = list(steps)
        self._total = len(self._steps)
        self._slots = slots
        self._calls = 0

    def one(self):
        self._calls += 1
        due = -(-self._calls * self._total // self._slots)
        while self._steps and self._total - len(self._steps) < due:
            self._steps.pop(0)()

    def rest(self):
        while self._steps:
            self._steps.pop(0)()


def _init_skew(s, nb, o_s, fhalo):
    @pl.when(s == 0)
    def _():
        o_s[...] = jnp.zeros_like(o_s)

    @pl.when((s == 0) | ((s + nb - 1) % nb == 0))
    def _():
        fhalo[...] = jnp.zeros_like(fhalo)


def _gdn_layer_kernel(p_ref, x_ref, convw_ref, alog_ref, dtb_ref, ogain_ref,
                      wout_ref, gmix_ref, gpre_ref, wup_ref, fconvw_ref, wdown_ref, gffn_ref,
                      y_ref,
                      xbuf, halo, q_s, k_s, v_s, beta_s, gc_s, gct_s, state,
                      o_s, m_s, xmid_s, h_s, ubuf, fhalo, act_s, f_s, *, tc, nb):
    nchunks = tc // CHUNK
    s_id = pl.program_id(0)
    cur = s_id % 2

    @pl.when(s_id % nb == 0)
    def _():
        halo[...] = jnp.zeros_like(halo)
        state[...] = jnp.zeros_like(state)

    _init_skew(s_id, nb, o_s, fhalo)
    ffn = _Interleaver(_ffn_steps(o_s.at[1 - cur], x_ref, wout_ref, gmix_ref, gpre_ref, wup_ref,
                                  fconvw_ref, wdown_ref, gffn_ref, y_ref,
                                  m_s, xmid_s, h_s, ubuf, fhalo, act_s, f_s, tc),
                       slots=GDN_SLOTS_PER_CHUNK * nchunks)

    xbuf[0:SUBLANES, :] = halo[...]
    xbuf[SUBLANES:SUBLANES + tc, :] = p_ref[:, 0:GDN_QKV]
    halo[...] = p_ref[tc - SUBLANES:tc, 0:GDN_QKV]

    def conv_silu(col):
        acc = convw_ref[GDN_CONV - 1:GDN_CONV, col:col + LANES] * xbuf[SUBLANES:SUBLANES + tc, col:col + LANES]
        for kk in range(GDN_CONV - 1):
            r0 = SUBLANES - (GDN_CONV - 1) + kk
            acc = acc + convw_ref[kk:kk + 1, col:col + LANES] * xbuf[r0:r0 + tc, col:col + LANES]
        return _silu(acc)

    def l2n(t):
        return t * lax.rsqrt(jnp.sum(t * t, axis=-1, keepdims=True) + EPS)

    for h in range(GDN_HEADS):
        c0 = h * GDN_DK
        q_s[:, c0:c0 + GDN_DK] = l2n(conv_silu(c0)) * (GDN_DK ** -0.5)
        k_s[:, c0:c0 + GDN_DK] = l2n(conv_silu(GDN_QK + c0))
        v_s[:, c0:c0 + GDN_DV] = conv_silu(2 * GDN_QK + h * GDN_DV)

    pg = p_ref[:, GDN_MAIN:GDN_MAIN + LANES]
    lane = lax.broadcasted_iota(jnp.int32, (tc, LANES), 1)
    beta_s[...] = jax.nn.sigmoid(pg)
    g = -jnp.exp(alog_ref[...]) * _softplus(pg + dtb_ref[...])
    g = jnp.where((lane >= GDN_HEADS) & (lane < 2 * GDN_HEADS), g, 0.0)
    gc = jnp.dot(_chunk_cumsum_matrix(tc), g, preferred_element_type=F32,
                 precision=lax.Precision.HIGHEST)
    gc_s[...] = gc
    gct = gc.T
    for c in range(nchunks):
        gct_s[c] = gct[:, c * CHUNK:(c + 1) * CHUNK]

    ii = lax.broadcasted_iota(jnp.int32, (CHUNK, CHUNK), 0)
    jj = lax.broadcasted_iota(jnp.int32, (CHUNK, CHUNK), 1)
    causal = ii >= jj
    strict = ii > jj
    eye = jnp.where(ii == jj, 1.0, 0.0).astype(F32)
    heads = range(GDN_HEADS)
    hsl = [slice(h * GDN_DK, (h + 1) * GDN_DK) for h in heads]

    for c in range(nchunks):
        r0 = c * CHUNK
        rows = pl.ds(r0, CHUNK)
        k = [k_s[rows, hsl[h]] for h in heads]
        beta = [beta_s[rows, h:h + 1] for h in heads]
        gcol = [gc_s[rows, GDN_HEADS + h:GDN_HEADS + h + 1] for h in heads]
        glast = [gc_s[pl.ds(r0 + CHUNK - 1, 1), GDN_HEADS + h:GDN_HEADS + h + 1] for h in heads]
        decay = [jnp.exp(jnp.where(causal, gcol[h] - gct_s[c, GDN_HEADS + h:GDN_HEADS + h + 1, :], -jnp.inf))
                 for h in heads]
        kb = [k[h] * beta[h] for h in heads]
        kk = [_dot_nt(jnp.concatenate([kb[h], q_s[rows, hsl[h]]], axis=0), k[h]) for h in heads]
        ffn.one()
        a = [jnp.where(strict, kk[h][:CHUNK] * decay[h], 0.0) for h in heads]
        qk = [jnp.where(causal, kk[h][CHUNK:] * decay[h], 0.0) for h in heads]
        tinv = [eye - a[h] for h in heads]
        apow = [_dot(a[h], a[h]) for h in heads]
        ffn.one()
        for _ in range(4):
            r = [_dot(jnp.concatenate([apow[h], tinv[h]], axis=0), apow[h]) for h in heads]
            ffn.one()
            apow = [r[h][:CHUNK] for h in heads]
            tinv = [tinv[h] + r[h][CHUNK:] for h in heads]
        tinv = [tinv[h] + _dot(tinv[h], apow[h]) for h in heads]
        ffn.one()
        egc = [jnp.exp(gcol[h]) for h in heads]
        rhs = [jnp.concatenate([v_s[rows, hsl[h]] * beta[h], kb[h] * egc[h]], axis=-1) for h in heads]
        sol = [_dot(tinv[h], rhs[h]) for h in heads]
        ffn.one()
        s = [state[h] for h in heads]
        ws = [_dot(jnp.concatenate([sol[h][:, GDN_DV:], q_s[rows, hsl[h]] * egc[h]], axis=0), s[h])
              for h in heads]
        ffn.one()
        v_new = [sol[h][:, :GDN_DV] - ws[h][:CHUNK] for h in heads]
        for h in heads:
            k_dec = k[h] * jnp.exp(glast[h] - gcol[h])
            state[h] = s[h] * jnp.exp(glast[h]) + _dot_tn(k_dec, v_new[h])
        for h in heads:
            o = ws[h][CHUNK:] + _dot(qk[h], v_new[h])
            z = p_ref[rows, GDN_QKV + h * GDN_DV:GDN_QKV + (h + 1) * GDN_DV]
            o_s[cur, rows, hsl[h]] = (_rms(o, ogain_ref[...]) * _silu(z)).astype(BF16)
        ffn.one()
    ffn.rest()


def _layer_specs(tc, nblk, p_cols):
    const = lambda s: (0, 0)
    prev = lambda s: (jnp.maximum(s - 1, 0), 0)
    p_spec = pl.BlockSpec((tc, p_cols), lambda s: (jnp.minimum(s, nblk - 1), 0))
    x_spec = pl.BlockSpec((tc, D_MODEL), prev)
    ffn_specs = [pl.BlockSpec((D_MODEL, D_MODEL), const, pipeline_mode=pl.Buffered(1)),
                 pl.BlockSpec((1, D_MODEL), const),
                 pl.BlockSpec((1, D_MODEL), const),
                 pl.BlockSpec((D_MODEL, 2 * D_FF), const, pipeline_mode=pl.Buffered(1)),
                 pl.BlockSpec((FFN_CONV, 2 * D_FF), const),
                 pl.BlockSpec((D_FF, D_MODEL), const, pipeline_mode=pl.Buffered(1)),
                 pl.BlockSpec((1, D_MODEL), const)]
    y_spec = pl.BlockSpec((tc, D_MODEL), prev)
    return const, p_spec, x_spec, ffn_specs, y_spec


def _gdn_layer(p, x2, conv_w, a_log, dt_bias, o_gain, ffn_args, seq, tc=SEQ_BLOCK):
    t = x2.shape[0]
    nb = seq // tc
    nblk = t // tc
    pad = jnp.zeros((LANES - 2 * GDN_HEADS,), F32)
    alog_row = jnp.concatenate([jnp.zeros((GDN_HEADS,), F32), a_log, pad])[None, :]
    dtb_row = jnp.concatenate([jnp.zeros((GDN_HEADS,), F32), dt_bias, pad])[None, :]
    const, p_spec, x_spec, ffn_specs, y_spec = _layer_specs(tc, nblk, GDN_IN_PAD)
    return pl.pallas_call(
        functools.partial(_gdn_layer_kernel, tc=tc, nb=nb),
        grid=(nblk + 1,),
        in_specs=[p_spec, x_spec,
                  pl.BlockSpec((GDN_CONV, GDN_QKV), const),
                  pl.BlockSpec((1, LANES), const),
                  pl.BlockSpec((1, LANES), const),
                  pl.BlockSpec((1, GDN_DV), const)] + ffn_specs,
        out_specs=y_spec,
        out_shape=jax.ShapeDtypeStruct((t, D_MODEL), F32),
        scratch_shapes=[pltpu.VMEM((tc + SUBLANES, GDN_QKV), F32),
                        pltpu.VMEM((SUBLANES, GDN_QKV), F32),
                        pltpu.VMEM((tc, GDN_QK), F32),
                        pltpu.VMEM((tc, GDN_QK), F32),
                        pltpu.VMEM((tc, GDN_V), F32),
                        pltpu.VMEM((tc, LANES), F32),
                        pltpu.VMEM((tc, LANES), F32),
                        pltpu.VMEM((tc // CHUNK, LANES, CHUNK), F32),
                        pltpu.VMEM((GDN_HEADS, GDN_DK, GDN_DV), F32),
                        ] + _FFN_SCRATCH(tc),
        compiler_params=_params("arbitrary"),
        name="gdn_layer",
    )(p, x2, conv_w, alog_row, dtb_row, o_gain[None, :], *ffn_args)


def _gla_layer_kernel(p_ref, x_ref, wgu_ref, gbias_ref, ogain_ref,
                      wout_ref, gmix_ref, gpre_ref, wup_ref, fconvw_ref, wdown_ref, gffn_ref,
                      y_ref,
                      bc_s, state,
                      o_s, m_s, xmid_s, h_s, ubuf, fhalo, act_s, f_s, *, tc, nb):
    nchunks = tc // CHUNK
    s_id = pl.program_id(0)
    cur = s_id % 2

    @pl.when(s_id % nb == 0)
    def _():
        state[...] = jnp.zeros_like(state)

    _init_skew(s_id, nb, o_s, fhalo)
    ffn = _Interleaver(_ffn_steps(o_s.at[1 - cur], x_ref, wout_ref, gmix_ref, gpre_ref, wup_ref,
                                  fconvw_ref, wdown_ref, gffn_ref, y_ref,
                                  m_s, xmid_s, h_s, ubuf, fhalo, act_s, f_s, tc),
                       slots=GLA_SLOTS_PER_CHUNK * nchunks)

    z = _dot(p_ref[:, GLA_MAIN:GLA_MAIN + LANES], wgu_ref[...]) + gbias_ref[...]
    log_a = -_softplus(-z) * (1.0 / GLA_TAU)
    bc_s[...] = jnp.dot(_chunk_cumsum_matrix(tc), log_a, preferred_element_type=F32,
                        precision=lax.Precision.HIGHEST)

    ii = lax.broadcasted_iota(jnp.int32, (CHUNK, CHUNK), 0)
    jj = lax.broadcasted_iota(jnp.int32, (CHUNK, CHUNK), 1)
    level_masks = [((ii // lv) == (jj // lv) + 1) & (((ii // lv) % 2) == 1) for lv in GLA_LEVELS]
    isub = lax.broadcasted_iota(jnp.int32, (GLA_SUB, GLA_DK), 0)
    jsub = lax.broadcasted_iota(jnp.int32, (GLA_SUB, CHUNK), 1)
    scale = GLA_DK ** -0.5
    heads = range(GLA_HEADS)

    def block_rows(b, lv, shift):
        parts = []
        for m in range(CHUNK // lv):
            r = min((m + shift) * lv, CHUNK - 1)
            parts.append(jnp.broadcast_to(b[r:r + 1], (lv, GLA_DK)))
        return jnp.concatenate(parts, axis=0)

    for c in range(nchunks):
        rows = pl.ds(c * CHUNK, CHUNK)
        q = [p_ref[rows, h * GLA_DK:(h + 1) * GLA_DK] * scale for h in heads]
        k = [p_ref[rows, GLA_K + h * GLA_DK:GLA_K + (h + 1) * GLA_DK] for h in heads]
        b = [bc_s[rows, h * GLA_DK:(h + 1) * GLA_DK] for h in heads]
        attn = [jnp.zeros((CHUNK, CHUNK), F32) for h in heads]
        for lv, mask in zip(GLA_LEVELS, level_masks):
            q_t = [q[h] * jnp.exp(b[h] - block_rows(b[h], lv, 0)) for h in heads]
            k_t = [k[h] * jnp.exp(block_rows(b[h], lv, 1) - b[h]) for h in heads]
            attn = [jnp.where(mask, _dot_nt(q_t[h], k_t[h]), attn[h]) for h in heads]
            ffn.one()
        diag = []
        for h in heads:
            blocks = []
            for blk in range(CHUNK // GLA_SUB):
                i0 = blk * GLA_SUB
                q_i = q[h][i0:i0 + GLA_SUB]
                b_i = b[h][i0:i0 + GLA_SUB]
                k_i = k[h][i0:i0 + GLA_SUB]
                acc = jnp.zeros((GLA_SUB, CHUNK), F32)
                for j in range(GLA_SUB):
                    e = jnp.exp(jnp.where(isub >= j, b_i - b_i[j:j + 1], -jnp.inf))
                    col = jnp.sum(q_i * k_i[j:j + 1] * e, axis=-1, keepdims=True)
                    acc = jnp.where(jsub == i0 + j, col, acc)
                blocks.append(acc)
            diag.append(jnp.concatenate(blocks, axis=0))
            if h % 2 == 1:
                ffn.one()
        st = [state[h] for h in heads]
        o = []
        for h in heads:
            v = p_ref[rows, 2 * GLA_K + h * GLA_DV:2 * GLA_K + (h + 1) * GLA_DV]
            blast = b[h][CHUNK - 1:CHUNK, :]
            o.append(_dot_nt(q[h] * jnp.exp(b[h]), st[h]) + _dot(attn[h] + diag[h], v))
            state[h] = st[h] * jnp.exp(blast) + _dot_tn(v, k[h] * jnp.exp(blast - b[h]))
        ffn.one()
        for h in heads:
            r = p_ref[rows, 2 * GLA_K + GLA_V + h * GLA_DV:2 * GLA_K + GLA_V + (h + 1) * GLA_DV]
            o_s[cur, rows, h * GLA_DV:(h + 1) * GLA_DV] = (
                _rms(o[h], ogain_ref[...]) * _silu(r)).astype(BF16)
    ffn.rest()


def _gla_layer(p, x2, w_gate_up, gate_bias, o_gain, ffn_args, seq, tc=SEQ_BLOCK):
    t = x2.shape[0]
    nb = seq // tc
    nblk = t // tc
    wgu = jnp.zeros((LANES, GLA_K), BF16).at[:GLA_GATE_RANK].set(w_gate_up.astype(BF16))
    const, p_spec, x_spec, ffn_specs, y_spec = _layer_specs(tc, nblk, GLA_IN_PAD)
    return pl.pallas_call(
        functools.partial(_gla_layer_kernel, tc=tc, nb=nb),
        grid=(nblk + 1,),
        in_specs=[p_spec, x_spec,
                  pl.BlockSpec((LANES, GLA_K), const),
                  pl.BlockSpec((1, GLA_K), const),
                  pl.BlockSpec((1, GLA_DV), const)] + ffn_specs,
        out_specs=y_spec,
        out_shape=jax.ShapeDtypeStruct((t, D_MODEL), F32),
        scratch_shapes=[pltpu.VMEM((tc, GLA_K), F32),
                        pltpu.VMEM((GLA_HEADS, GLA_DV, GLA_DK), F32),
                        ] + _FFN_SCRATCH(tc),
        compiler_params=_params("arbitrary"),
        name="gla_layer",
    )(p, x2, wgu, gate_bias[None, :], o_gain[None, :], *ffn_args)


def _pad_cols(w, n):
    return jnp.pad(w, ((0, 0), (0, n - w.shape[1])))


def kernel(x, gdn_w_in, gdn_conv_w, gdn_a_log, gdn_dt_bias, gdn_o_norm, gdn_w_out,
           gla_w_in, gla_w_gate_up, gla_gate_bias, gla_o_norm, gla_w_out,
           mix_pre_norm, mix_post_norm, ffn_pre_norm, ffn_post_norm,
           ffn_w_up, ffn_conv_w, ffn_w_down):
    batch, seq, d = x.shape
    depth = mix_pre_norm.shape[0]
    x2 = x.reshape(batch * seq, d)
    for layer in range(depth):
        j = layer // 2
        g_pre = mix_pre_norm[layer][None, :]
        w_out = gdn_w_out[j] if layer % 2 == 0 else gla_w_out[j]
        ffn_args = (w_out.astype(BF16), mix_post_norm[layer][None, :], ffn_pre_norm[layer][None, :],
                    ffn_w_up[layer].astype(BF16), ffn_conv_w[layer], ffn_w_down[layer].astype(BF16),
                    ffn_post_norm[layer][None, :])
        if layer % 2 == 0:
            p = _norm_matmul(x2, g_pre, _pad_cols(gdn_w_in[j], GDN_IN_PAD).astype(BF16))
            x2 = _gdn_layer(p, x2, gdn_conv_w[j], gdn_a_log[j], gdn_dt_bias[j], gdn_o_norm[j],
                            ffn_args, seq)
        else:
            p = _norm_matmul(x2, g_pre, _pad_cols(gla_w_in[j], GLA_IN_PAD).astype(BF16))
            x2 = _gla_layer(p, x2, gla_w_gate_up[j], gla_gate_bias[j], gla_o_norm[j], ffn_args, seq)
    return x2.reshape(batch, seq, d)
```

```python
import functools

import jax
import jax.numpy as jnp
from jax import lax
from jax.experimental import pallas as pl
from jax.experimental.pallas import tpu as pltpu

EPS = 1e-6
CHUNK = 64
LANES = 128
SUBLANES = 8
MXU_DIM = 256
D_MODEL = 1024

GDN_HEADS = 8
GDN_DK = 128
GDN_DV = 128
GDN_CONV = 4
GDN_QK = GDN_HEADS * GDN_DK
GDN_V = GDN_HEADS * GDN_DV
GDN_QKV = 2 * GDN_QK + GDN_V
GDN_MAIN = GDN_QKV + GDN_V
GDN_IN_PAD = GDN_MAIN + LANES

GLA_HEADS = 4
GLA_DK = 128
GLA_DV = 256
GLA_GATE_RANK = 16
GLA_TAU = 16.0
GLA_K = GLA_HEADS * GLA_DK
GLA_V = GLA_HEADS * GLA_DV
GLA_MAIN = 2 * GLA_K + 2 * GLA_V
GLA_IN_PAD = GLA_MAIN + LANES
GLA_SUB = 8
GLA_LEVELS = (32, 16, 8)

D_FF = 2816
FFN_CONV = 3
FFN_TILE = MXU_DIM
FFN_DOWN_SPLIT = (3, 3, 3, 2)
SEQ_BLOCK = 256
GDN_SLOTS_PER_CHUNK = 10
GLA_SLOTS_PER_CHUNK = 6

VMEM_LIMIT = 56 * 1024 * 1024

BF16 = jnp.bfloat16
F32 = jnp.float32


def _dot(a, b):
    return jnp.dot(a.astype(BF16), b.astype(BF16), preferred_element_type=F32)


def _dot_nt(a, b):
    return lax.dot_general(a.astype(BF16), b.astype(BF16), (((1,), (1,)), ((), ())),
                           preferred_element_type=F32)


def _dot_tn(a, b):
    return lax.dot_general(a.astype(BF16), b.astype(BF16), (((0,), (0,)), ((), ())),
                           preferred_element_type=F32)


def _rms(x, g):
    return x * lax.rsqrt(jnp.mean(x * x, axis=-1, keepdims=True) + EPS) * g


def _silu(x):
    return x * jax.nn.sigmoid(x)


def _softplus(x):
    return jnp.maximum(x, 0.0) + jnp.log1p(jnp.exp(-jnp.abs(x)))


def _chunk_cumsum(x):
    n = x.shape[0]
    i = lax.broadcasted_iota(jnp.int32, (n, n), 0)
    j = lax.broadcasted_iota(jnp.int32, (n, n), 1)
    tri = jnp.where((j <= i) & ((i // CHUNK) == (j // CHUNK)), 1.0, 0.0).astype(BF16)
    hi = x.astype(BF16)
    rest = x - hi.astype(F32)
    mid = rest.astype(BF16)
    lo = (rest - mid.astype(F32)).astype(BF16)
    dot = lambda t: jnp.dot(tri, t, preferred_element_type=F32)
    return dot(hi) + dot(mid) + dot(lo)


def _params(*sem):
    return pltpu.CompilerParams(dimension_semantics=sem, vmem_limit_bytes=VMEM_LIMIT)


_FFN_SCRATCH = lambda tm: [
    pltpu.VMEM((2, tm, D_MODEL), BF16),
    pltpu.VMEM((tm, D_MODEL), F32),
    pltpu.VMEM((tm, D_MODEL), F32),
    pltpu.VMEM((tm, D_MODEL), BF16),
    pltpu.VMEM((2, tm + SUBLANES, FFN_TILE), F32),
    pltpu.VMEM((SUBLANES, 2 * D_FF), F32),
    pltpu.VMEM((tm, D_FF), BF16),
    pltpu.VMEM((tm, D_MODEL), F32),
]


def _ffn_steps(o_prev, x_ref, wout_ref, gmix_ref, gpre_ref, wup_ref, fconvw_ref, wdown_ref, gffn_ref,
               y_ref, m_s, xmid_s, h_s, ubuf, fhalo, act_s, f_s, tm):
    steps = []

    def out_proj(n):
        cs = slice(n * MXU_DIM, (n + 1) * MXU_DIM)
        m_s[:, cs] = jnp.dot(o_prev[...], wout_ref[:, cs], preferred_element_type=F32)

    def norms():
        xm = x_ref[...] + _rms(m_s[...], gmix_ref[...])
        xmid_s[...] = xm
        h_s[...] = _rms(xm, gpre_ref[...]).astype(BF16)

    def conv_tile(col, buf):
        u = jnp.dot(h_s[...], wup_ref[:, col:col + FFN_TILE], preferred_element_type=F32)
        ubuf[buf, 0:SUBLANES, :] = fhalo[:, col:col + FFN_TILE]
        ubuf[buf, SUBLANES:SUBLANES + tm, :] = u
        fhalo[:, col:col + FFN_TILE] = u[tm - SUBLANES:tm]
        acc = fconvw_ref[FFN_CONV - 1:FFN_CONV, col:col + FFN_TILE] * u
        for kk in range(FFN_CONV - 1):
            r0 = SUBLANES - (FFN_CONV - 1) + kk
            acc = acc + fconvw_ref[kk:kk + 1, col:col + FFN_TILE] * ubuf[buf, r0:r0 + tm, :]
        return acc

    def up(j):
        gate = conv_tile(j * FFN_TILE, 0)
        val = conv_tile(D_FF + j * FFN_TILE, 1)
        act_s[:, j * FFN_TILE:(j + 1) * FFN_TILE] = (_silu(gate) * val).astype(BF16)

    def down(k0, k1):
        part = jnp.dot(act_s[:, k0:k1], wdown_ref[k0:k1, :], preferred_element_type=F32)
        if k0 == 0:
            f_s[...] = part
        else:
            f_s[...] += part

    def finish():
        y_ref[...] = xmid_s[...] + _rms(f_s[...], gffn_ref[...])

    for n in range(D_MODEL // MXU_DIM):
        steps.append(functools.partial(out_proj, n))
    steps.append(norms)
    for j in range(D_FF // FFN_TILE):
        steps.append(functools.partial(up, j))
    k0 = 0
    for width in FFN_DOWN_SPLIT:
        steps.append(functools.partial(down, k0, k0 + width * FFN_TILE))
        k0 += width * FFN_TILE
    steps.append(finish)
    return steps


class _Interleaver:
    def __init__(self, steps, slots):
        self._steps = list(steps)
        self._total = len(self._steps)
        self._slots = slots
        self._calls = 0

    def one(self):
        self._calls += 1
        due = -(-self._calls * self._total // self._slots)
        while self._steps and self._total - len(self._steps) < due:
            self._steps.pop(0)()

    def rest(self):
        while self._steps:
            self._steps.pop(0)()


def _init_skew(s, nb, o_s, fhalo):
    @pl.when(s == 0)
    def _():
        o_s[...] = jnp.zeros_like(o_s)

    @pl.when((s == 0) | ((s + nb - 1) % nb == 0))
    def _():
        fhalo[...] = jnp.zeros_like(fhalo)


def _inproj_steps(xn_ref, gin_ref, win_ref, hin_s, dst, n_cols, tile):
    def norm():
        hin_s[...] = _rms(xn_ref[...], gin_ref[...]).astype(BF16)

    def tile_step(c0, c1):
        dst[:, c0:c1] = jnp.dot(hin_s[...], win_ref[:, c0:c1], preferred_element_type=F32)

    steps = [norm]
    for c0 in range(0, n_cols, tile):
        steps.append(functools.partial(tile_step, c0, min(c0 + tile, n_cols)))
    return steps


def _spread(steps, extra):
    out, done = [], 0
    for i, step in enumerate(steps):
        out.append(step)
        due = (i + 1) * len(extra) // len(steps)
        out.extend(extra[done:due])
        done = due
    return out


def _first_inproj(s, x_ref, gin_ref, win_ref, p_s, dst0):
    @pl.when(s == 0)
    def _():
        p_s[...] = jnp.zeros_like(p_s)
        dst0[...] = _dot(_rms(x_ref[...], gin_ref[...]), win_ref[...])


def _gdn_layer_kernel(xn_ref, x_ref, gin_ref, win_ref, convw_ref, alog_ref, dtb_ref, ogain_ref,
                      wout_ref, gmix_ref, gpre_ref, wup_ref, fconvw_ref, wdown_ref, gffn_ref,
                      y_ref,
                      p_s, pnext_s, hin_s, q_s, k_s, v_s, beta_s, gc_s, gct_s, state,
                      o_s, m_s, xmid_s, h_s, ubuf, fhalo, act_s, f_s, *, tc, nb):
    nchunks = tc // CHUNK
    s_id = pl.program_id(0)
    cur = s_id % 2
    body = pl.ds(SUBLANES, tc)

    _init_skew(s_id, nb, o_s, fhalo)
    _first_inproj(s_id, x_ref, gin_ref, win_ref, p_s, p_s.at[body, :])

    @pl.when(s_id % nb == 0)
    def _():
        state[...] = jnp.zeros_like(state)
        p_s[0:SUBLANES, :] = jnp.zeros((SUBLANES, GDN_IN_PAD), F32)

    ffn = _Interleaver(_ffn_steps(o_s.at[1 - cur], x_ref, wout_ref, gmix_ref, gpre_ref, wup_ref,
                                  fconvw_ref, wdown_ref, gffn_ref, y_ref,
                                  m_s, xmid_s, h_s, ubuf, fhalo, act_s, f_s, tc),
                       slots=GDN_SLOTS_PER_CHUNK * nchunks)
    inproj = _Interleaver(_inproj_steps(xn_ref, gin_ref, win_ref, hin_s, pnext_s,
                                        GDN_IN_PAD, 4 * LANES), slots=GDN_HEADS)

    def conv_silu(col):
        acc = convw_ref[GDN_CONV - 1:GDN_CONV, col:col + LANES] * p_s[body, col:col + LANES]
        for kk in range(GDN_CONV - 1):
            r0 = SUBLANES - (GDN_CONV - 1) + kk
            acc = acc + convw_ref[kk:kk + 1, col:col + LANES] * p_s[pl.ds(r0, tc), col:col + LANES]
        return _silu(acc)

    def l2n(t):
        return t * lax.rsqrt(jnp.sum(t * t, axis=-1, keepdims=True) + EPS)

    for h in range(GDN_HEADS):
        c0 = h * GDN_DK
        q_s[:, c0:c0 + GDN_DK] = l2n(conv_silu(c0)) * (GDN_DK ** -0.5)
        k_s[:, c0:c0 + GDN_DK] = l2n(conv_silu(GDN_QK + c0))
        v_s[:, c0:c0 + GDN_DV] = conv_silu(2 * GDN_QK + h * GDN_DV)
        inproj.one()
    inproj.rest()

    pg = p_s[body, GDN_MAIN:GDN_MAIN + LANES]
    lane = lax.broadcasted_iota(jnp.int32, (tc, LANES), 1)
    beta_s[...] = jax.nn.sigmoid(pg)
    g = -jnp.exp(alog_ref[...]) * _softplus(pg + dtb_ref[...])
    g = jnp.where((lane >= GDN_HEADS) & (lane < 2 * GDN_HEADS), g, 0.0)
    gc = _chunk_cumsum(g)
    gc_s[...] = gc
    gct = gc.T
    for c in range(nchunks):
        gct_s[c] = gct[:, c * CHUNK:(c + 1) * CHUNK]

    ii = lax.broadcasted_iota(jnp.int32, (CHUNK, CHUNK), 0)
    jj = lax.broadcasted_iota(jnp.int32, (CHUNK, CHUNK), 1)
    causal = ii >= jj
    strict = ii > jj
    eye = jnp.where(ii == jj, 1.0, 0.0).astype(F32)
    heads = range(GDN_HEADS)
    hsl = [slice(h * GDN_DK, (h + 1) * GDN_DK) for h in heads]

    for c in range(nchunks):
        r0 = c * CHUNK
        rows = pl.ds(r0, CHUNK)
        k = [k_s[rows, hsl[h]] for h in heads]
        beta = [beta_s[rows, h:h + 1] for h in heads]
        gcol = [gc_s[rows, GDN_HEADS + h:GDN_HEADS + h + 1] for h in heads]
        glast = [gc_s[pl.ds(r0 + CHUNK - 1, 1), GDN_HEADS + h:GDN_HEADS + h + 1] for h in heads]
        decay = [jnp.exp(jnp.where(causal, gcol[h] - gct_s[c, GDN_HEADS + h:GDN_HEADS + h + 1, :], -jnp.inf))
                 for h in heads]
        kb = [k[h] * beta[h] for h in heads]
        kk = [_dot_nt(jnp.concatenate([kb[h], q_s[rows, hsl[h]]], axis=0), k[h]) for h in heads]
        ffn.one()
        a = [jnp.where(strict, kk[h][:CHUNK] * decay[h], 0.0) for h in heads]
        qk = [jnp.where(causal, kk[h][CHUNK:] * decay[h], 0.0) for h in heads]
        tinv = [eye - a[h] for h in heads]
        apow = [_dot(a[h], a[h]) for h in heads]
        ffn.one()
        for _ in range(4):
            r = [_dot(jnp.concatenate([apow[h], tinv[h]], axis=0), apow[h]) for h in heads]
            ffn.one()
            apow = [r[h][:CHUNK] for h in heads]
            tinv = [tinv[h] + r[h][CHUNK:] for h in heads]
        tinv = [tinv[h] + _dot(tinv[h], apow[h]) for h in heads]
        ffn.one()
        egc = [jnp.exp(gcol[h]) for h in heads]
        rhs = [jnp.concatenate([v_s[rows, hsl[h]] * beta[h], kb[h] * egc[h]], axis=-1) for h in heads]
        sol = [_dot(tinv[h], rhs[h]) for h in heads]
        ffn.one()
        s = [state[h] for h in heads]
        ws = [_dot(jnp.concatenate([sol[h][:, GDN_DV:], q_s[rows, hsl[h]] * egc[h]], axis=0), s[h])
              for h in heads]
        ffn.one()
        v_new = [sol[h][:, :GDN_DV] - ws[h][:CHUNK] for h in heads]
        for h in heads:
            k_dec = k[h] * jnp.exp(glast[h] - gcol[h])
            state[h] = s[h] * jnp.exp(glast[h]) + _dot_tn(k_dec, v_new[h])
        for h in heads:
            o = ws[h][CHUNK:] + _dot(qk[h], v_new[h])
            z = p_s[pl.ds(SUBLANES + r0, CHUNK), GDN_QKV + h * GDN_DV:GDN_QKV + (h + 1) * GDN_DV]
            o_s[cur, rows, hsl[h]] = (_rms(o, ogain_ref[...]) * _silu(z)).astype(BF16)
        ffn.one()
    ffn.rest()
    p_s[0:SUBLANES, :] = p_s[tc:tc + SUBLANES, :]
    p_s[body, :] = pnext_s[...]


def _layer_specs(tc, nblk, p_cols):
    const = lambda s: (0, 0)
    prev = lambda s: (jnp.maximum(s - 1, 0), 0)
    p_spec = [pl.BlockSpec((tc, D_MODEL), lambda s: (jnp.minimum(s + 1, nblk - 1), 0))]
    x_spec = [pl.BlockSpec((tc, D_MODEL), prev),
              pl.BlockSpec((1, D_MODEL), const),
              pl.BlockSpec((D_MODEL, p_cols), const, pipeline_mode=pl.Buffered(1))]
    ffn_specs = [pl.BlockSpec((D_MODEL, D_MODEL), const, pipeline_mode=pl.Buffered(1)),
                 pl.BlockSpec((1, D_MODEL), const),
                 pl.BlockSpec((1, D_MODEL), const),
                 pl.BlockSpec((D_MODEL, 2 * D_FF), const, pipeline_mode=pl.Buffered(1)),
                 pl.BlockSpec((FFN_CONV, 2 * D_FF), const),
                 pl.BlockSpec((D_FF, D_MODEL), const, pipeline_mode=pl.Buffered(1)),
                 pl.BlockSpec((1, D_MODEL), const)]
    y_spec = pl.BlockSpec((tc, D_MODEL), prev)
    return const, p_spec, x_spec, ffn_specs, y_spec


def _gdn_layer(x2, g_in, w_in, conv_w, a_log, dt_bias, o_gain, ffn_args, seq, tc=SEQ_BLOCK):
    t = x2.shape[0]
    nb = seq // tc
    nblk = t // tc
    pad = jnp.zeros((LANES - 2 * GDN_HEADS,), F32)
    alog_row = jnp.concatenate([jnp.zeros((GDN_HEADS,), F32), a_log, pad])[None, :]
    dtb_row = jnp.concatenate([jnp.zeros((GDN_HEADS,), F32), dt_bias, pad])[None, :]
    const, p_spec, x_spec, ffn_specs, y_spec = _layer_specs(tc, nblk, GDN_IN_PAD)
    return pl.pallas_call(
        functools.partial(_gdn_layer_kernel, tc=tc, nb=nb),
        grid=(nblk + 1,),
        in_specs=p_spec + x_spec + [
                  pl.BlockSpec((GDN_CONV, GDN_QKV), const),
                  pl.BlockSpec((1, LANES), const),
                  pl.BlockSpec((1, LANES), const),
                  pl.BlockSpec((1, GDN_DV), const)] + ffn_specs,
        out_specs=y_spec,
        out_shape=jax.ShapeDtypeStruct((t, D_MODEL), F32),
        scratch_shapes=[pltpu.VMEM((tc + SUBLANES, GDN_IN_PAD), F32),
                        pltpu.VMEM((tc, GDN_IN_PAD), F32),
                        pltpu.VMEM((tc, D_MODEL), BF16),
                        pltpu.VMEM((tc, GDN_QK), F32),
                        pltpu.VMEM((tc, GDN_QK), F32),
                        pltpu.VMEM((tc, GDN_V), F32),
                        pltpu.VMEM((tc, LANES), F32),
                        pltpu.VMEM((tc, LANES), F32),
                        pltpu.VMEM((tc // CHUNK, LANES, CHUNK), F32),
                        pltpu.VMEM((GDN_HEADS, GDN_DK, GDN_DV), F32),
                        ] + _FFN_SCRATCH(tc),
        compiler_params=_params("arbitrary"),
        name="gdn_layer",
    )(x2, x2, g_in, w_in, conv_w, alog_row, dtb_row, o_gain[None, :], *ffn_args)


def _gla_layer_kernel(xn_ref, x_ref, gin_ref, win_ref, wgu_ref, gbias_ref, ogain_ref,
                      wout_ref, gmix_ref, gpre_ref, wup_ref, fconvw_ref, wdown_ref, gffn_ref,
                      y_ref,
                      p_ref, pnext_s, hin_s, bc_s, state,
                      o_s, m_s, xmid_s, h_s, ubuf, fhalo, act_s, f_s, *, tc, nb):
    nchunks = tc // CHUNK
    s_id = pl.program_id(0)
    cur = s_id % 2

    @pl.when(s_id % nb == 0)
    def _():
        state[...] = jnp.zeros_like(state)

    _init_skew(s_id, nb, o_s, fhalo)
    _first_inproj(s_id, x_ref, gin_ref, win_ref, p_ref, p_ref)
    ffn = _Interleaver(
        _spread(_ffn_steps(o_s.at[1 - cur], x_ref, wout_ref, gmix_ref, gpre_ref, wup_ref,
                           fconvw_ref, wdown_ref, gffn_ref, y_ref,
                           m_s, xmid_s, h_s, ubuf, fhalo, act_s, f_s, tc),
                _inproj_steps(xn_ref, gin_ref, win_ref, hin_s, pnext_s, GLA_IN_PAD, 4 * LANES)),
        slots=2 + GLA_SLOTS_PER_CHUNK * nchunks)

    z = _dot(p_ref[:, GLA_MAIN:GLA_MAIN + LANES], wgu_ref[...]) + gbias_ref[...]
    log_a = -_softplus(-z) * (1.0 / GLA_TAU)
    ffn.one()
    bc_s[...] = _chunk_cumsum(log_a)
    ffn.one()

    ii = lax.broadcasted_iota(jnp.int32, (CHUNK, CHUNK), 0)
    jj = lax.broadcasted_iota(jnp.int32, (CHUNK, CHUNK), 1)
    level_masks = [((ii // lv) == (jj // lv) + 1) & (((ii // lv) % 2) == 1) for lv in GLA_LEVELS]
    isub = lax.broadcasted_iota(jnp.int32, (GLA_SUB, GLA_DK), 0)
    jsub = lax.broadcasted_iota(jnp.int32, (GLA_SUB, CHUNK), 1)
    scale = GLA_DK ** -0.5
    heads = range(GLA_HEADS)

    def block_rows(b, lv, shift):
        parts = []
        for m in range(CHUNK // lv):
            r = min((m + shift) * lv, CHUNK - 1)
            parts.append(jnp.broadcast_to(b[r:r + 1], (lv, GLA_DK)))
        return jnp.concatenate(parts, axis=0)

    for c in range(nchunks):
        rows = pl.ds(c * CHUNK, CHUNK)
        q = [p_ref[rows, h * GLA_DK:(h + 1) * GLA_DK] * scale for h in heads]
        k = [p_ref[rows, GLA_K + h * GLA_DK:GLA_K + (h + 1) * GLA_DK] for h in heads]
        b = [bc_s[rows, h * GLA_DK:(h + 1) * GLA_DK] for h in heads]
        attn = [jnp.zeros((CHUNK, CHUNK), F32) for h in heads]
        for lv, mask in zip(GLA_LEVELS, level_masks):
            q_t = [q[h] * jnp.exp(b[h] - block_rows(b[h], lv, 0)) for h in heads]
            k_t = [k[h] * jnp.exp(block_rows(b[h], lv, 1) - b[h]) for h in heads]
            attn = [jnp.where(mask, _dot_nt(q_t[h], k_t[h]), attn[h]) for h in heads]
            ffn.one()
        diag = []
        for h in heads:
            blocks = []
            for blk in range(CHUNK // GLA_SUB):
                i0 = blk * GLA_SUB
                q_i = q[h][i0:i0 + GLA_SUB]
                b_i = b[h][i0:i0 + GLA_SUB]
                k_i = k[h][i0:i0 + GLA_SUB]
                acc = jnp.zeros((GLA_SUB, CHUNK), F32)
                for j in range(GLA_SUB):
                    e = jnp.exp(jnp.where(isub >= j, b_i - b_i[j:j + 1], -jnp.inf))
                    col = jnp.sum(q_i * k_i[j:j + 1] * e, axis=-1, keepdims=True)
                    acc = jnp.where(jsub == i0 + j, col, acc)
                blocks.append(acc)
            diag.append(jnp.concatenate(blocks, axis=0))
            if h % 2 == 1:
                ffn.one()
        st = [state[h] for h in heads]
        o = []
        for h in heads:
            v = p_ref[rows, 2 * GLA_K + h * GLA_DV:2 * GLA_K + (h + 1) * GLA_DV]
            blast = b[h][CHUNK - 1:CHUNK, :]
            o.append(_dot_nt(q[h] * jnp.exp(b[h]), st[h]) + _dot(attn[h] + diag[h], v))
            state[h] = st[h] * jnp.exp(blast) + _dot_tn(v, k[h] * jnp.exp(blast - b[h]))
        ffn.one()
        for h in heads:
            r = p_ref[rows, 2 * GLA_K + GLA_V + h * GLA_DV:2 * GLA_K + GLA_V + (h + 1) * GLA_DV]
            o_s[cur, rows, h * GLA_DV:(h + 1) * GLA_DV] = (
                _rms(o[h], ogain_ref[...]) * _silu(r)).astype(BF16)
    ffn.rest()
    p_ref[...] = pnext_s[...]


def _gla_layer(x2, g_in, w_in, w_gate_up, gate_bias, o_gain, ffn_args, seq, tc=SEQ_BLOCK):
    t = x2.shape[0]
    nb = seq // tc
    nblk = t // tc
    wgu = jnp.zeros((LANES, GLA_K), BF16).at[:GLA_GATE_RANK].set(w_gate_up.astype(BF16))
    const, p_spec, x_spec, ffn_specs, y_spec = _layer_specs(tc, nblk, GLA_IN_PAD)
    return pl.pallas_call(
        functools.partial(_gla_layer_kernel, tc=tc, nb=nb),
        grid=(nblk + 1,),
        in_specs=p_spec + x_spec + [
                  pl.BlockSpec((LANES, GLA_K), const),
                  pl.BlockSpec((1, GLA_K), const),
                  pl.BlockSpec((1, GLA_DV), const)] + ffn_specs,
        out_specs=y_spec,
        out_shape=jax.ShapeDtypeStruct((t, D_MODEL), F32),
        scratch_shapes=[pltpu.VMEM((tc, GLA_IN_PAD), F32),
                        pltpu.VMEM((tc, GLA_IN_PAD), F32),
                        pltpu.VMEM((tc, D_MODEL), BF16),
                        pltpu.VMEM((tc, GLA_K), F32),
                        pltpu.VMEM((GLA_HEADS, GLA_DV, GLA_DK), F32),
                        ] + _FFN_SCRATCH(tc),
        compiler_params=_params("arbitrary"),
        name="gla_layer",
    )(x2, x2, g_in, w_in, wgu, gate_bias[None, :], o_gain[None, :], *ffn_args)


def _pad_cols(w, n):
    return jnp.pad(w, ((0, 0), (0, n - w.shape[1])))


def kernel(x, gdn_w_in, gdn_conv_w, gdn_a_log, gdn_dt_bias, gdn_o_norm, gdn_w_out,
           gla_w_in, gla_w_gate_up, gla_gate_bias, gla_o_norm, gla_w_out,
           mix_pre_norm, mix_post_norm, ffn_pre_norm, ffn_post_norm,
           ffn_w_up, ffn_conv_w, ffn_w_down):
    batch, seq, d = x.shape
    depth = mix_pre_norm.shape[0]
    x2 = x.reshape(batch * seq, d)
    for layer in range(depth):
        j = layer // 2
        g_pre = mix_pre_norm[layer][None, :]
        w_out = gdn_w_out[j] if layer % 2 == 0 else gla_w_out[j]
        ffn_args = (w_out.astype(BF16), mix_post_norm[layer][None, :], ffn_pre_norm[layer][None, :],
                    ffn_w_up[layer].astype(BF16), ffn_conv_w[layer], ffn_w_down[layer].astype(BF16),
                    ffn_post_norm[layer][None, :])
        if layer % 2 == 0:
            w_in = _pad_cols(gdn_w_in[j], GDN_IN_PAD).astype(BF16)
            x2 = _gdn_layer(x2, g_pre, w_in, gdn_conv_w[j], gdn_a_log[j], gdn_dt_bias[j], gdn_o_norm[j],
                            ffn_args, seq)
        else:
            w_in = _pad_cols(gla_w_in[j], GLA_IN_PAD).astype(BF16)
            x2 = _gla_layer(x2, g_pre, w_in, gla_w_gate_up[j], gla_gate_bias[j], gla_o_norm[j], ffn_args, seq)
    return x2.reshape(batch, seq, d)
```

```python
import functools

import jax
import jax.numpy as jnp
from jax import lax
from jax.experimental import pallas as pl
from jax.experimental.pallas import tpu as pltpu

EPS = 1e-6
CHUNK = 64
LANES = 128
SUBLANES = 8
MXU_DIM = 256
D_MODEL = 1024

GDN_HEADS = 8
GDN_DK = 128
GDN_DV = 128
GDN_CONV = 4
GDN_QK = GDN_HEADS * GDN_DK
GDN_V = GDN_HEADS * GDN_DV
GDN_PAIRS = GDN_HEADS // 2
GDN_QKV = 2 * GDN_QK + GDN_V
GDN_MAIN = GDN_QKV + GDN_V
GDN_IN_PAD = GDN_MAIN + LANES

GLA_HEADS = 4
GLA_DK = 128
GLA_DV = 256
GLA_GATE_RANK = 16
GLA_TAU = 16.0
GLA_K = GLA_HEADS * GLA_DK
GLA_V = GLA_HEADS * GLA_DV
GLA_MAIN = 2 * GLA_K + 2 * GLA_V
GLA_IN_PAD = GLA_MAIN + LANES
GLA_SUB = 8
GLA_LEVELS = (32, 16, 8)

D_FF = 2816
FFN_CONV = 3
FFN_TILE = MXU_DIM
FFN_DOWN_SPLIT = (3, 3, 3, 2)
SEQ_BLOCK = 256
GDN_CHUNK_GROUP = 4
GDN_GROUP_SLOTS = 8 + 2 * GDN_CHUNK_GROUP
GLA_SLOTS_PER_CHUNK = 6

VMEM_LIMIT = 56 * 1024 * 1024

BF16 = jnp.bfloat16
F32 = jnp.float32


def _dot(a, b):
    return jnp.dot(a.astype(BF16), b.astype(BF16), preferred_element_type=F32)


def _dot_nt(a, b):
    return lax.dot_general(a.astype(BF16), b.astype(BF16), (((1,), (1,)), ((), ())),
                           preferred_element_type=F32)


def _dot_tn(a, b):
    return lax.dot_general(a.astype(BF16), b.astype(BF16), (((0,), (0,)), ((), ())),
                           preferred_element_type=F32)


def _rms(x, g):
    return x * lax.rsqrt(jnp.mean(x * x, axis=-1, keepdims=True) + EPS) * g


def _silu(x):
    return x * jax.nn.sigmoid(x)


def _softplus(x):
    return jnp.maximum(x, 0.0) + jnp.log1p(jnp.exp(-jnp.abs(x)))


def _chunk_cumsum(x):
    n = x.shape[0]
    i = lax.broadcasted_iota(jnp.int32, (n, n), 0)
    j = lax.broadcasted_iota(jnp.int32, (n, n), 1)
    tri = jnp.where((j <= i) & ((i // CHUNK) == (j // CHUNK)), 1.0, 0.0).astype(BF16)
    hi = x.astype(BF16)
    rest = x - hi.astype(F32)
    mid = rest.astype(BF16)
    lo = (rest - mid.astype(F32)).astype(BF16)
    dot = lambda t: jnp.dot(tri, t, preferred_element_type=F32)
    return dot(hi) + dot(mid) + dot(lo)


def _params(*sem):
    return pltpu.CompilerParams(dimension_semantics=sem, vmem_limit_bytes=VMEM_LIMIT)


_FFN_SCRATCH = lambda tm: [
    pltpu.VMEM((2, tm, D_MODEL), BF16),
    pltpu.VMEM((tm, D_MODEL), F32),
    pltpu.VMEM((tm, D_MODEL), F32),
    pltpu.VMEM((tm, D_MODEL), BF16),
    pltpu.VMEM((2, tm + SUBLANES, FFN_TILE), F32),
    pltpu.VMEM((SUBLANES, 2 * D_FF), F32),
    pltpu.VMEM((tm, D_FF), BF16),
    pltpu.VMEM((tm, D_MODEL), F32),
]


def _ffn_steps(o_prev, x_ref, wout_ref, gmix_ref, gpre_ref, wup_ref, fconvw_ref, wdown_ref, gffn_ref,
               y_ref, m_s, xmid_s, h_s, ubuf, fhalo, act_s, f_s, tm):
    steps = []

    def out_proj(n):
        cs = slice(n * MXU_DIM, (n + 1) * MXU_DIM)
        m_s[:, cs] = jnp.dot(o_prev[...], wout_ref[:, cs], preferred_element_type=F32)

    def norms():
        xm = x_ref[...] + _rms(m_s[...], gmix_ref[...])
        xmid_s[...] = xm
        h_s[...] = _rms(xm, gpre_ref[...]).astype(BF16)

    def conv_tile(col, buf):
        u = jnp.dot(h_s[...], wup_ref[:, col:col + FFN_TILE], preferred_element_type=F32)
        ubuf[buf, 0:SUBLANES, :] = fhalo[:, col:col + FFN_TILE]
        ubuf[buf, SUBLANES:SUBLANES + tm, :] = u
        fhalo[:, col:col + FFN_TILE] = u[tm - SUBLANES:tm]
        acc = fconvw_ref[FFN_CONV - 1:FFN_CONV, col:col + FFN_TILE] * u
        for kk in range(FFN_CONV - 1):
            r0 = SUBLANES - (FFN_CONV - 1) + kk
            acc = acc + fconvw_ref[kk:kk + 1, col:col + FFN_TILE] * ubuf[buf, r0:r0 + tm, :]
        return acc

    def up(j):
        gate = conv_tile(j * FFN_TILE, 0)
        val = conv_tile(D_FF + j * FFN_TILE, 1)
        act_s[:, j * FFN_TILE:(j + 1) * FFN_TILE] = (_silu(gate) * val).astype(BF16)

    def down(k0, k1):
        part = jnp.dot(act_s[:, k0:k1], wdown_ref[k0:k1, :], preferred_element_type=F32)
        if k0 == 0:
            f_s[...] = part
        else:
            f_s[...] += part

    def finish():
        y_ref[...] = xmid_s[...] + _rms(f_s[...], gffn_ref[...])

    for n in range(D_MODEL // MXU_DIM):
        steps.append(functools.partial(out_proj, n))
    steps.append(norms)
    for j in range(D_FF // FFN_TILE):
        steps.append(functools.partial(up, j))
    k0 = 0
    for width in FFN_DOWN_SPLIT:
        steps.append(functools.partial(down, k0, k0 + width * FFN_TILE))
        k0 += width * FFN_TILE
    steps.append(finish)
    return steps


class _Interleaver:
    def __init__(self, steps, slots):
        self._steps = list(steps)
        self._total = len(self._steps)
        self._slots = slots
        self._calls = 0

    def one(self):
        self._calls += 1
        due = -(-self._calls * self._total // self._slots)
        while self._steps and self._total - len(self._steps) < due:
            self._steps.pop(0)()

    def rest(self):
        while self._steps:
            self._steps.pop(0)()


def _init_skew(s, nb, o_s, fhalo):
    @pl.when(s == 0)
    def _():
        o_s[...] = jnp.zeros_like(o_s)

    @pl.when((s == 0) | ((s + nb - 1) % nb == 0))
    def _():
        fhalo[...] = jnp.zeros_like(fhalo)


def _inproj_steps(xn_ref, gin_ref, win_ref, hin_s, dst, n_cols, tile):
    def norm():
        hin_s[...] = _rms(xn_ref[...], gin_ref[...]).astype(BF16)

    def tile_step(c0, c1):
        dst[:, c0:c1] = jnp.dot(hin_s[...], win_ref[:, c0:c1], preferred_element_type=F32)

    steps = [norm]
    for c0 in range(0, n_cols, tile):
        steps.append(functools.partial(tile_step, c0, min(c0 + tile, n_cols)))
    return steps


def _spread(steps, extra):
    out, done = [], 0
    for i, step in enumerate(steps):
        out.append(step)
        due = (i + 1) * len(extra) // len(steps)
        out.extend(extra[done:due])
        done = due
    return out


def _first_inproj(s, x_ref, gin_ref, win_ref, p_s, dst0):
    @pl.when(s == 0)
    def _():
        p_s[...] = jnp.zeros_like(p_s)
        dst0[...] = _dot(_rms(x_ref[...], gin_ref[...]), win_ref[...])


def _gdn_layer_kernel(xn_ref, x_ref, gin_ref, win_ref, convw_ref, alog_ref, dtb_ref, ogain_ref,
                      wout_ref, gmix_ref, gpre_ref, wup_ref, fconvw_ref, wdown_ref, gffn_ref,
                      y_ref,
                      p_s, pnext_s, hin_s, q_s, k_s, v_s, beta_s, gc_s, grow_s, state,
                      o_s, m_s, xmid_s, h_s, ubuf, fhalo, act_s, f_s, *, tc, nb):
    nchunks = tc // CHUNK
    s_id = pl.program_id(0)
    cur = s_id % 2
    body = pl.ds(SUBLANES, tc)

    _init_skew(s_id, nb, o_s, fhalo)
    _first_inproj(s_id, x_ref, gin_ref, win_ref, p_s, p_s.at[body, :])

    @pl.when(s_id % nb == 0)
    def _():
        state[...] = jnp.zeros_like(state)
        p_s[0:SUBLANES, :] = jnp.zeros((SUBLANES, GDN_IN_PAD), F32)

    ffn = _Interleaver(_ffn_steps(o_s.at[1 - cur], x_ref, wout_ref, gmix_ref, gpre_ref, wup_ref,
                                  fconvw_ref, wdown_ref, gffn_ref, y_ref,
                                  m_s, xmid_s, h_s, ubuf, fhalo, act_s, f_s, tc),
                       slots=GDN_GROUP_SLOTS * (nchunks // GDN_CHUNK_GROUP))
    inproj = _Interleaver(_inproj_steps(xn_ref, gin_ref, win_ref, hin_s, pnext_s,
                                        GDN_IN_PAD, 4 * LANES), slots=GDN_HEADS)

    def conv_silu(col):
        acc = convw_ref[GDN_CONV - 1:GDN_CONV, col:col + LANES] * p_s[body, col:col + LANES]
        for kk in range(GDN_CONV - 1):
            r0 = SUBLANES - (GDN_CONV - 1) + kk
            acc = acc + convw_ref[kk:kk + 1, col:col + LANES] * p_s[pl.ds(r0, tc), col:col + LANES]
        return _silu(acc)

    def l2n(t):
        return t * lax.rsqrt(jnp.sum(t * t, axis=-1, keepdims=True) + EPS)

    for h in range(GDN_HEADS):
        c0 = h * GDN_DK
        q_s[:, c0:c0 + GDN_DK] = l2n(conv_silu(c0)) * (GDN_DK ** -0.5)
        k_s[:, c0:c0 + GDN_DK] = l2n(conv_silu(GDN_QK + c0))
        v_s[:, c0:c0 + GDN_DV] = conv_silu(2 * GDN_QK + h * GDN_DV)
        inproj.one()
    inproj.rest()

    pg = p_s[body, GDN_MAIN:GDN_MAIN + LANES]
    lane = lax.broadcasted_iota(jnp.int32, (tc, LANES), 1)
    beta_s[...] = jax.nn.sigmoid(pg)
    g = -jnp.exp(alog_ref[...]) * _softplus(pg + dtb_ref[...])
    g = jnp.where((lane >= GDN_HEADS) & (lane < 2 * GDN_HEADS), g, 0.0)
    gc = _chunk_cumsum(g)
    gc_s[...] = gc
    gct = gc.T
    lane_row = lax.broadcasted_iota(jnp.int32, (1, LANES), 1)
    for m in range(tc // LANES):
        tile = gct[GDN_HEADS:2 * GDN_HEADS, m * LANES:(m + 1) * LANES]
        swapped = pltpu.roll(tile, CHUNK, axis=1)
        for pr in range(GDN_PAIRS):
            a, b = 2 * pr, 2 * pr + 1
            grow_s[pl.ds((2 * m) * GDN_PAIRS + pr, 1), :] = jnp.where(
                lane_row < CHUNK, tile[a:a + 1], swapped[b:b + 1])
            grow_s[pl.ds((2 * m + 1) * GDN_PAIRS + pr, 1), :] = jnp.where(
                lane_row < CHUNK, swapped[a:a + 1], tile[b:b + 1])

    ii = lax.broadcasted_iota(jnp.int32, (CHUNK, 2 * CHUNK), 0)
    ll = lax.broadcasted_iota(jnp.int32, (CHUNK, 2 * CHUNK), 1)
    first = ll < CHUNK
    jj = jnp.where(first, ll, ll - CHUNK)
    causal = ii >= jj
    strict = ii > jj
    eye = jnp.where(ii == jj, 1.0, 0.0).astype(F32)
    heads = range(GDN_HEADS)
    pairs = range(GDN_PAIRS)
    hsl = [slice(h * GDN_DK, (h + 1) * GDN_DK) for h in heads]

    def blockdiag(x):
        return jnp.concatenate([jnp.where(first, x, 0.0), jnp.where(first, 0.0, x)], axis=0)

    def blockdiag_wide(xa, xb):
        z = jnp.zeros_like(xa)
        return jnp.concatenate([jnp.concatenate([xa, z], axis=1), jnp.concatenate([z, xb], axis=1)], axis=0)

    def rows_of(c):
        return pl.ds(c * CHUNK, CHUNK)

    def head_cols(ref, c, h):
        return ref[rows_of(c), hsl[h]]

    def gcol(c, h):
        return gc_s[rows_of(c), GDN_HEADS + h:GDN_HEADS + h + 1]

    def kbeta(c, h):
        return head_cols(k_s, c, h) * beta_s[rows_of(c), h:h + 1]

    for g0 in range(0, nchunks, GDN_CHUNK_GROUP):
        units = [(c, pr) for c in range(g0, g0 + GDN_CHUNK_GROUP) for pr in pairs]
        decay = [jnp.exp(jnp.where(causal, jnp.where(first, gcol(c, 2 * pr), gcol(c, 2 * pr + 1))
                                   - grow_s[pl.ds(c * GDN_PAIRS + pr, 1), :], -jnp.inf)) for c, pr in units]
        kk = [_dot_nt(jnp.concatenate(
                  [jnp.concatenate([kbeta(c, 2 * pr), kbeta(c, 2 * pr + 1)], axis=1),
                   jnp.concatenate([head_cols(q_s, c, 2 * pr), head_cols(q_s, c, 2 * pr + 1)], axis=1)], axis=0),
                  blockdiag_wide(head_cols(k_s, c, 2 * pr), head_cols(k_s, c, 2 * pr + 1))) for c, pr in units]
        ffn.one()
        nu = range(len(units))
        a = [jnp.where(strict, kk[u][:CHUNK] * decay[u], 0.0) for u in nu]
        qk = [jnp.where(causal, kk[u][CHUNK:] * decay[u], 0.0) for u in nu]
        tinv = [eye - a[u] for u in nu]
        apow = [_dot(a[u], blockdiag(a[u])) for u in nu]
        ffn.one()
        for _ in range(4):
            r = [_dot(jnp.concatenate([apow[u], tinv[u]], axis=0), blockdiag(apow[u])) for u in nu]
            ffn.one()
            apow = [r[u][:CHUNK] for u in nu]
            tinv = [tinv[u] + r[u][CHUNK:] for u in nu]
        tinv = [tinv[u] + _dot(tinv[u], blockdiag(apow[u])) for u in nu]
        ffn.one()

        def rhs(c, h):
            return jnp.concatenate([head_cols(v_s, c, h) * beta_s[rows_of(c), h:h + 1],
                                    kbeta(c, h) * jnp.exp(gcol(c, h))], axis=-1)

        sol_p = [_dot(tinv[u], blockdiag_wide(rhs(c, 2 * pr), rhs(c, 2 * pr + 1)))
                 for u, (c, pr) in enumerate(units)]
        ffn.one()
        width = GDN_DV + GDN_DK
        for ci, c in enumerate(range(g0, g0 + GDN_CHUNK_GROUP)):
            rows = rows_of(c)
            sol = [sol_p[ci * GDN_PAIRS + h // 2][:, (h % 2) * width:(h % 2 + 1) * width] for h in heads]
            s = [state[h] for h in heads]
            ws = [_dot(jnp.concatenate([sol[h][:, GDN_DV:], head_cols(q_s, c, h) * jnp.exp(gcol(c, h))], axis=0),
                       s[h]) for h in heads]
            ffn.one()
            v_new = [sol[h][:, :GDN_DV] - ws[h][:CHUNK] for h in heads]
            for h in heads:
                glast = gc_s[pl.ds(c * CHUNK + CHUNK - 1, 1), GDN_HEADS + h:GDN_HEADS + h + 1]
                k_dec = head_cols(k_s, c, h) * jnp.exp(glast - gcol(c, h))
                state[h] = s[h] * jnp.exp(glast) + _dot_tn(k_dec, v_new[h])
            o_p = [_dot(qk[ci * GDN_PAIRS + pr], blockdiag_wide(v_new[2 * pr], v_new[2 * pr + 1])) for pr in pairs]
            for h in heads:
                o = ws[h][CHUNK:] + o_p[h // 2][:, (h % 2) * GDN_DV:(h % 2 + 1) * GDN_DV]
                z = p_s[pl.ds(SUBLANES + c * CHUNK, CHUNK), GDN_QKV + h * GDN_DV:GDN_QKV + (h + 1) * GDN_DV]
                o_s[cur, rows, hsl[h]] = (_rms(o, ogain_ref[...]) * _silu(z)).astype(BF16)
            ffn.one()
    ffn.rest()
    p_s[0:SUBLANES, :] = p_s[tc:tc + SUBLANES, :]
    p_s[body, :] = pnext_s[...]


def _stacked(rows, cols, index, resident=False):
    mode = dict(pipeline_mode=pl.Buffered(1)) if resident else {}
    return pl.BlockSpec((None, rows, cols), lambda s: (index, 0, 0), **mode)


def _layer_specs(tc, nblk, p_cols, layer, j):
    prev = lambda s: (jnp.maximum(s - 1, 0), 0)
    x_specs = [pl.BlockSpec((tc, D_MODEL), lambda s: (jnp.minimum(s + 1, nblk - 1), 0)),
               pl.BlockSpec((tc, D_MODEL), prev),
               _stacked(1, D_MODEL, layer),
               _stacked(D_MODEL, p_cols, j, resident=True)]
    ffn_specs = [_stacked(D_MODEL, D_MODEL, j, resident=True),
                 _stacked(1, D_MODEL, layer),
                 _stacked(1, D_MODEL, layer),
                 _stacked(D_MODEL, 2 * D_FF, layer, resident=True),
                 _stacked(FFN_CONV, 2 * D_FF, layer),
                 _stacked(D_FF, D_MODEL, layer, resident=True),
                 _stacked(1, D_MODEL, layer)]
    y_spec = pl.BlockSpec((tc, D_MODEL), prev)
    return x_specs, ffn_specs, y_spec


def _gdn_layer(x2, g_in, w_in, conv_w, alog_rows, dtb_rows, o_gain, ffn_args, seq, layer, j, tc=SEQ_BLOCK):
    t = x2.shape[0]
    nb = seq // tc
    nblk = t // tc
    x_specs, ffn_specs, y_spec = _layer_specs(tc, nblk, GDN_IN_PAD, layer, j)
    return pl.pallas_call(
        functools.partial(_gdn_layer_kernel, tc=tc, nb=nb),
        grid=(nblk + 1,),
        in_specs=x_specs + [
                  _stacked(GDN_CONV, GDN_QKV, j),
                  _stacked(1, LANES, j),
                  _stacked(1, LANES, j),
                  _stacked(1, GDN_DV, j)] + ffn_specs,
        out_specs=y_spec,
        out_shape=jax.ShapeDtypeStruct((t, D_MODEL), F32),
        scratch_shapes=[pltpu.VMEM((tc + SUBLANES, GDN_IN_PAD), F32),
                        pltpu.VMEM((tc, GDN_IN_PAD), F32),
                        pltpu.VMEM((tc, D_MODEL), BF16),
                        pltpu.VMEM((tc, GDN_QK), F32),
                        pltpu.VMEM((tc, GDN_QK), F32),
                        pltpu.VMEM((tc, GDN_V), F32),
                        pltpu.VMEM((tc, LANES), F32),
                        pltpu.VMEM((tc, LANES), F32),
                        pltpu.VMEM((tc // CHUNK * GDN_PAIRS, LANES), F32),
                        pltpu.VMEM((GDN_HEADS, GDN_DK, GDN_DV), F32),
                        ] + _FFN_SCRATCH(tc),
        compiler_params=_params("arbitrary"),
        name="gdn_layer",
    )(x2, x2, g_in, w_in, conv_w, alog_rows, dtb_rows, o_gain, *ffn_args)


def _gla_layer_kernel(xn_ref, x_ref, gin_ref, win_ref, wgu_ref, gbias_ref, ogain_ref,
                      wout_ref, gmix_ref, gpre_ref, wup_ref, fconvw_ref, wdown_ref, gffn_ref,
                      y_ref,
                      p_ref, pnext_s, hin_s, bc_s, state,
                      o_s, m_s, xmid_s, h_s, ubuf, fhalo, act_s, f_s, *, tc, nb):
    nchunks = tc // CHUNK
    s_id = pl.program_id(0)
    cur = s_id % 2

    @pl.when(s_id % nb == 0)
    def _():
        state[...] = jnp.zeros_like(state)

    _init_skew(s_id, nb, o_s, fhalo)
    _first_inproj(s_id, x_ref, gin_ref, win_ref, p_ref, p_ref)
    ffn = _Interleaver(
        _spread(_ffn_steps(o_s.at[1 - cur], x_ref, wout_ref, gmix_ref, gpre_ref, wup_ref,
                           fconvw_ref, wdown_ref, gffn_ref, y_ref,
                           m_s, xmid_s, h_s, ubuf, fhalo, act_s, f_s, tc),
                _inproj_steps(xn_ref, gin_ref, win_ref, hin_s, pnext_s, GLA_IN_PAD, 4 * LANES)),
        slots=2 + GLA_SLOTS_PER_CHUNK * nchunks)

    z = _dot(p_ref[:, GLA_MAIN:GLA_MAIN + LANES], wgu_ref[...]) + gbias_ref[...]
    log_a = -_softplus(-z) * (1.0 / GLA_TAU)
    ffn.one()
    bc_s[...] = _chunk_cumsum(log_a)
    ffn.one()

    ii = lax.broadcasted_iota(jnp.int32, (CHUNK, CHUNK), 0)
    jj = lax.broadcasted_iota(jnp.int32, (CHUNK, CHUNK), 1)
    level_masks = [((ii // lv) == (jj // lv) + 1) & (((ii // lv) % 2) == 1) for lv in GLA_LEVELS]
    isub = lax.broadcasted_iota(jnp.int32, (GLA_SUB, GLA_DK), 0)
    jsub = lax.broadcasted_iota(jnp.int32, (GLA_SUB, CHUNK), 1)
    scale = GLA_DK ** -0.5
    heads = range(GLA_HEADS)

    def block_rows(b, lv, shift):
        parts = []
        for m in range(CHUNK // lv):
            r = min((m + shift) * lv, CHUNK - 1)
            parts.append(jnp.broadcast_to(b[r:r + 1], (lv, GLA_DK)))
        return jnp.concatenate(parts, axis=0)

    for c in range(nchunks):
        rows = pl.ds(c * CHUNK, CHUNK)
        q = [p_ref[rows, h * GLA_DK:(h + 1) * GLA_DK] * scale for h in heads]
        k = [p_ref[rows, GLA_K + h * GLA_DK:GLA_K + (h + 1) * GLA_DK] for h in heads]
        b = [bc_s[rows, h * GLA_DK:(h + 1) * GLA_DK] for h in heads]
        attn = [jnp.zeros((CHUNK, CHUNK), F32) for h in heads]
        for lv, mask in zip(GLA_LEVELS, level_masks):
            q_t = [q[h] * jnp.exp(b[h] - block_rows(b[h], lv, 0)) for h in heads]
            k_t = [k[h] * jnp.exp(block_rows(b[h], lv, 1) - b[h]) for h in heads]
            attn = [jnp.where(mask, _dot_nt(q_t[h], k_t[h]), attn[h]) for h in heads]
            ffn.one()
        diag = []
        for h in heads:
            blocks = []
            for blk in range(CHUNK // GLA_SUB):
                i0 = blk * GLA_SUB
                q_i = q[h][i0:i0 + GLA_SUB]
                b_i = b[h][i0:i0 + GLA_SUB]
                k_i = k[h][i0:i0 + GLA_SUB]
                acc = jnp.zeros((GLA_SUB, CHUNK), F32)
                for j in range(GLA_SUB):
                    e = jnp.exp(jnp.where(isub >= j, b_i - b_i[j:j + 1], -jnp.inf))
                    col = jnp.sum(q_i * k_i[j:j + 1] * e, axis=-1, keepdims=True)
                    acc = jnp.where(jsub == i0 + j, col, acc)
                blocks.append(acc)
            diag.append(jnp.concatenate(blocks, axis=0))
            if h % 2 == 1:
                ffn.one()
        st = [state[h] for h in heads]
        o = []
        for h in heads:
            v = p_ref[rows, 2 * GLA_K + h * GLA_DV:2 * GLA_K + (h + 1) * GLA_DV]
            blast = b[h][CHUNK - 1:CHUNK, :]
            o.append(_dot_nt(q[h] * jnp.exp(b[h]), st[h]) + _dot(attn[h] + diag[h], v))
            state[h] = st[h] * jnp.exp(blast) + _dot_tn(v, k[h] * jnp.exp(blast - b[h]))
        ffn.one()
        for h in heads:
            r = p_ref[rows, 2 * GLA_K + GLA_V + h * GLA_DV:2 * GLA_K + GLA_V + (h + 1) * GLA_DV]
            o_s[cur, rows, h * GLA_DV:(h + 1) * GLA_DV] = (
                _rms(o[h], ogain_ref[...]) * _silu(r)).astype(BF16)
    ffn.rest()
    p_ref[...] = pnext_s[...]


def _gla_layer(x2, g_in, w_in, wgu, gate_bias, o_gain, ffn_args, seq, layer, j, tc=SEQ_BLOCK):
    t = x2.shape[0]
    nb = seq // tc
    nblk = t // tc
    x_specs, ffn_specs, y_spec = _layer_specs(tc, nblk, GLA_IN_PAD, layer, j)
    return pl.pallas_call(
        functools.partial(_gla_layer_kernel, tc=tc, nb=nb),
        grid=(nblk + 1,),
        in_specs=x_specs + [
                  _stacked(LANES, GLA_K, j),
                  _stacked(1, GLA_K, j),
                  _stacked(1, GLA_DV, j)] + ffn_specs,
        out_specs=y_spec,
        out_shape=jax.ShapeDtypeStruct((t, D_MODEL), F32),
        scratch_shapes=[pltpu.VMEM((tc, GLA_IN_PAD), F32),
                        pltpu.VMEM((tc, GLA_IN_PAD), F32),
                        pltpu.VMEM((tc, D_MODEL), BF16),
                        pltpu.VMEM((tc, GLA_K), F32),
                        pltpu.VMEM((GLA_HEADS, GLA_DV, GLA_DK), F32),
                        ] + _FFN_SCRATCH(tc),
        compiler_params=_params("arbitrary"),
        name="gla_layer",
    )(x2, x2, g_in, w_in, wgu, gate_bias, o_gain, *ffn_args)


def _pad_last(w, n):
    return jnp.pad(w, [(0, 0)] * (w.ndim - 1) + [(0, n - w.shape[-1])])


def _rows(v):
    return v[:, None, :]


def kernel(x, gdn_w_in, gdn_conv_w, gdn_a_log, gdn_dt_bias, gdn_o_norm, gdn_w_out,
           gla_w_in, gla_w_gate_up, gla_gate_bias, gla_o_norm, gla_w_out,
           mix_pre_norm, mix_post_norm, ffn_pre_norm, ffn_post_norm,
           ffn_w_up, ffn_conv_w, ffn_w_down):
    batch, seq, d = x.shape
    depth = mix_pre_norm.shape[0]
    x2 = x.reshape(batch * seq, d)
    g_pre = _rows(mix_pre_norm)
    ffn_tail = (_rows(mix_post_norm), _rows(ffn_pre_norm), ffn_w_up.astype(BF16), ffn_conv_w,
                ffn_w_down.astype(BF16), _rows(ffn_post_norm))
    gdn_w_in_p = _pad_last(gdn_w_in, GDN_IN_PAD).astype(BF16)
    gla_w_in_p = _pad_last(gla_w_in, GLA_IN_PAD).astype(BF16)
    gdn_w_out_b = gdn_w_out.astype(BF16)
    gla_w_out_b = gla_w_out.astype(BF16)
    on_decay_lanes = lambda v: _rows(jnp.pad(v, ((0, 0), (GDN_HEADS, LANES - 2 * GDN_HEADS))))
    alog_rows = on_decay_lanes(gdn_a_log)
    dtb_rows = on_decay_lanes(gdn_dt_bias)
    wgu = jnp.pad(gla_w_gate_up, ((0, 0), (0, LANES - GLA_GATE_RANK), (0, 0))).astype(BF16)
    for layer in range(depth):
        j = layer // 2
        if layer % 2 == 0:
            x2 = _gdn_layer(x2, g_pre, gdn_w_in_p, gdn_conv_w, alog_rows, dtb_rows, _rows(gdn_o_norm),
                            (gdn_w_out_b,) + ffn_tail, seq, layer, j)
        else:
            x2 = _gla_layer(x2, g_pre, gla_w_in_p, wgu, _rows(gla_gate_bias), _rows(gla_o_norm),
                            (gla_w_out_b,) + ffn_tail, seq, layer, j)
    return x2.reshape(batch, seq, d)
```

```python
import functools

import jax
import jax.numpy as jnp
from jax import lax
from jax.experimental import pallas as pl
from jax.experimental.pallas import tpu as pltpu

EPS = 1e-6
CHUNK = 64
LANES = 128
SUBLANES = 8
MXU_DIM = 256
D_MODEL = 1024

GDN_HEADS = 8
GDN_DK = 128
GDN_DV = 128
GDN_CONV = 4
GDN_QK = GDN_HEADS * GDN_DK
GDN_V = GDN_HEADS * GDN_DV
GDN_PAIRS = GDN_HEADS // 2
GDN_QKV = 2 * GDN_QK + GDN_V
GDN_MAIN = GDN_QKV + GDN_V
GDN_IN_PAD = GDN_MAIN + LANES

GLA_HEADS = 4
GLA_DK = 128
GLA_DV = 256
GLA_GATE_RANK = 16
GLA_TAU = 16.0
GLA_K = GLA_HEADS * GLA_DK
GLA_V = GLA_HEADS * GLA_DV
GLA_MAIN = 2 * GLA_K + 2 * GLA_V
GLA_IN_PAD = GLA_MAIN + LANES
GLA_SUB = 8
GLA_LEVELS = (32, 16, 8)

D_FF = 2816
FFN_CONV = 3
FFN_TILE = MXU_DIM
SEQ_BLOCK = 256
GDN_CHUNK_GROUP = 4
GDN_GROUP_SLOTS = 8 + 2 * GDN_CHUNK_GROUP
GLA_SLOTS_PER_CHUNK = 6

VMEM_LIMIT = 56 * 1024 * 1024

BF16 = jnp.bfloat16
F32 = jnp.float32


def _dot(a, b):
    return jnp.dot(a.astype(BF16), b.astype(BF16), preferred_element_type=F32)


def _dot_nt(a, b):
    return lax.dot_general(a.astype(BF16), b.astype(BF16), (((1,), (1,)), ((), ())),
                           preferred_element_type=F32)


def _dot_tn(a, b):
    return lax.dot_general(a.astype(BF16), b.astype(BF16), (((0,), (0,)), ((), ())),
                           preferred_element_type=F32)


def _rms(x, g):
    return x * lax.rsqrt(jnp.mean(x * x, axis=-1, keepdims=True) + EPS) * g


def _silu(x):
    return x * jax.nn.sigmoid(x)


def _softplus(x):
    return jnp.maximum(x, 0.0) + jnp.log1p(jnp.exp(-jnp.abs(x)))


def _chunk_cumsum(x):
    n = x.shape[0]
    i = lax.broadcasted_iota(jnp.int32, (n, n), 0)
    j = lax.broadcasted_iota(jnp.int32, (n, n), 1)
    tri = jnp.where((j <= i) & ((i // CHUNK) == (j // CHUNK)), 1.0, 0.0).astype(BF16)
    hi = x.astype(BF16)
    rest = x - hi.astype(F32)
    mid = rest.astype(BF16)
    lo = (rest - mid.astype(F32)).astype(BF16)
    dot = lambda t: jnp.dot(tri, t, preferred_element_type=F32)
    return dot(hi) + dot(mid) + dot(lo)


def _params(*sem):
    return pltpu.CompilerParams(dimension_semantics=sem, vmem_limit_bytes=VMEM_LIMIT)


_FFN_SCRATCH = lambda tm: [
    pltpu.VMEM((2, tm, D_MODEL), BF16),
    pltpu.VMEM((tm, D_MODEL), F32),
    pltpu.VMEM((tm, D_MODEL), F32),
    pltpu.VMEM((tm, D_MODEL), BF16),
    pltpu.VMEM((2, tm + SUBLANES, FFN_TILE), F32),
    pltpu.VMEM((SUBLANES, 2 * D_FF), F32),
    pltpu.VMEM((tm, D_FF), BF16),
    pltpu.VMEM((tm, D_MODEL), F32),
]


def _ffn_steps(o_prev, x_ref, wout_ref, gmix_ref, gpre_ref, wup_ref, fconvw_ref, wdown_ref, gffn_ref,
               y_ref, m_s, xmid_s, h_s, ubuf, fhalo, act_s, f_s, tm):
    steps = []

    def out_proj(n):
        cs = slice(n * MXU_DIM, (n + 1) * MXU_DIM)
        m_s[:, cs] = jnp.dot(o_prev[...], wout_ref[:, cs], preferred_element_type=F32)

    def norms():
        xm = x_ref[...] + _rms(m_s[...], gmix_ref[...])
        xmid_s[...] = xm
        h_s[...] = _rms(xm, gpre_ref[...]).astype(BF16)

    def conv_tile(col, buf):
        u = jnp.dot(h_s[...], wup_ref[:, col:col + FFN_TILE], preferred_element_type=F32)
        ubuf[buf, 0:SUBLANES, :] = fhalo[:, col:col + FFN_TILE]
        ubuf[buf, SUBLANES:SUBLANES + tm, :] = u
        fhalo[:, col:col + FFN_TILE] = u[tm - SUBLANES:tm]
        acc = fconvw_ref[FFN_CONV - 1:FFN_CONV, col:col + FFN_TILE] * u
        for kk in range(FFN_CONV - 1):
            r0 = SUBLANES - (FFN_CONV - 1) + kk
            acc = acc + fconvw_ref[kk:kk + 1, col:col + FFN_TILE] * ubuf[buf, r0:r0 + tm, :]
        return acc

    def up(j):
        gate = conv_tile(j * FFN_TILE, 0)
        val = conv_tile(D_FF + j * FFN_TILE, 1)
        act_s[:, j * FFN_TILE:(j + 1) * FFN_TILE] = (_silu(gate) * val).astype(BF16)

    def down(n):
        cs = slice(n * MXU_DIM, (n + 1) * MXU_DIM)
        f_s[:, cs] = jnp.dot(act_s[...], wdown_ref[:, cs], preferred_element_type=F32)

    def finish():
        y_ref[...] = xmid_s[...] + _rms(f_s[...], gffn_ref[...])

    for n in range(D_MODEL // MXU_DIM):
        steps.append(functools.partial(out_proj, n))
    steps.append(norms)
    for j in range(D_FF // FFN_TILE):
        steps.append(functools.partial(up, j))
    for n in range(D_MODEL // MXU_DIM):
        steps.append(functools.partial(down, n))
    steps.append(finish)
    return steps


class _Interleaver:
    def __init__(self, steps, slots):
        self._steps = list(steps)
        self._total = len(self._steps)
        self._slots = slots
        self._calls = 0

    def one(self):
        self._calls += 1
        due = -(-self._calls * self._total // self._slots)
        while self._steps and self._total - len(self._steps) < due:
            self._steps.pop(0)()

    def rest(self):
        while self._steps:
            self._steps.pop(0)()


def _init_skew(s, nb, o_s, fhalo):
    @pl.when(s == 0)
    def _():
        o_s[...] = jnp.zeros_like(o_s)

    @pl.when((s == 0) | ((s + nb - 1) % nb == 0))
    def _():
        fhalo[...] = jnp.zeros_like(fhalo)


def _inproj_steps(xn_ref, gin_ref, win_ref, hin_s, dst, n_cols, tile):
    def norm():
        hin_s[...] = _rms(xn_ref[...], gin_ref[...]).astype(BF16)

    def tile_step(c0, c1):
        dst[:, c0:c1] = jnp.dot(hin_s[...], win_ref[:, c0:c1], preferred_element_type=F32)

    steps = [norm]
    for c0 in range(0, n_cols, tile):
        steps.append(functools.partial(tile_step, c0, min(c0 + tile, n_cols)))
    return steps


def _spread(steps, extra):
    out, done = [], 0
    for i, step in enumerate(steps):
        out.append(step)
        due = (i + 1) * len(extra) // len(steps)
        out.extend(extra[done:due])
        done = due
    return out


def _first_inproj(s, x_ref, gin_ref, win_ref, p_s, dst0):
    @pl.when(s == 0)
    def _():
        p_s[...] = jnp.zeros_like(p_s)
        dst0[...] = _dot(_rms(x_ref[...], gin_ref[...]), win_ref[...])


def _gdn_layer_kernel(xn_ref, x_ref, gin_ref, win_ref, convw_ref, alog_ref, dtb_ref, ogain_ref,
                      wout_ref, gmix_ref, gpre_ref, wup_ref, fconvw_ref, wdown_ref, gffn_ref,
                      y_ref,
                      p_s, pnext_s, hin_s, q_s, k_s, v_s, beta_s, gc_s, grow_s, state,
                      o_s, m_s, xmid_s, h_s, ubuf, fhalo, act_s, f_s, *, tc, nb):
    nchunks = tc // CHUNK
    s_id = pl.program_id(0)
    cur = s_id % 2
    body = pl.ds(SUBLANES, tc)

    _init_skew(s_id, nb, o_s, fhalo)
    _first_inproj(s_id, x_ref, gin_ref, win_ref, p_s, p_s.at[body, :])

    @pl.when(s_id % nb == 0)
    def _():
        state[...] = jnp.zeros_like(state)
        p_s[0:SUBLANES, :] = jnp.zeros((SUBLANES, GDN_IN_PAD), F32)

    ffn = _Interleaver(_ffn_steps(o_s.at[1 - cur], x_ref, wout_ref, gmix_ref, gpre_ref, wup_ref,
                                  fconvw_ref, wdown_ref, gffn_ref, y_ref,
                                  m_s, xmid_s, h_s, ubuf, fhalo, act_s, f_s, tc),
                       slots=GDN_GROUP_SLOTS * (nchunks // GDN_CHUNK_GROUP))

    def conv_silu(col):
        acc = convw_ref[GDN_CONV - 1:GDN_CONV, col:col + LANES] * p_s[body, col:col + LANES]
        for kk in range(GDN_CONV - 1):
            r0 = SUBLANES - (GDN_CONV - 1) + kk
            acc = acc + convw_ref[kk:kk + 1, col:col + LANES] * p_s[pl.ds(r0, tc), col:col + LANES]
        return _silu(acc)

    def l2n(t):
        return t * lax.rsqrt(jnp.sum(t * t, axis=-1, keepdims=True) + EPS)

    for h in range(GDN_HEADS):
        c0 = h * GDN_DK
        q_s[:, c0:c0 + GDN_DK] = l2n(conv_silu(c0)) * (GDN_DK ** -0.5)
        k_s[:, c0:c0 + GDN_DK] = l2n(conv_silu(GDN_QK + c0))
        v_s[:, c0:c0 + GDN_DV] = conv_silu(2 * GDN_QK + h * GDN_DV)
    for step in _inproj_steps(xn_ref, gin_ref, win_ref, hin_s, pnext_s, GDN_IN_PAD, MXU_DIM):
        step()

    pg = p_s[body, GDN_MAIN:GDN_MAIN + LANES]
    lane = lax.broadcasted_iota(jnp.int32, (tc, LANES), 1)
    beta_s[...] = jax.nn.sigmoid(pg)
    g = -jnp.exp(alog_ref[...]) * _softplus(pg + dtb_ref[...])
    g = jnp.where((lane >= GDN_HEADS) & (lane < 2 * GDN_HEADS), g, 0.0)
    gc = _chunk_cumsum(g)
    gc_s[...] = gc
    gct = gc.T
    lane_row = lax.broadcasted_iota(jnp.int32, (1, LANES), 1)
    for m in range(tc // LANES):
        tile = gct[GDN_HEADS:2 * GDN_HEADS, m * LANES:(m + 1) * LANES]
        swapped = pltpu.roll(tile, CHUNK, axis=1)
        for pr in range(GDN_PAIRS):
            a, b = 2 * pr, 2 * pr + 1
            grow_s[pl.ds((2 * m) * GDN_PAIRS + pr, 1), :] = jnp.where(
                lane_row < CHUNK, tile[a:a + 1], swapped[b:b + 1])
            grow_s[pl.ds((2 * m + 1) * GDN_PAIRS + pr, 1), :] = jnp.where(
                lane_row < CHUNK, swapped[a:a + 1], tile[b:b + 1])

    ii = lax.broadcasted_iota(jnp.int32, (CHUNK, 2 * CHUNK), 0)
    ll = lax.broadcasted_iota(jnp.int32, (CHUNK, 2 * CHUNK), 1)
    first = ll < CHUNK
    jj = jnp.where(first, ll, ll - CHUNK)
    causal = ii >= jj
    strict = ii > jj
    eye = jnp.where(ii == jj, 1.0, 0.0).astype(F32)
    heads = range(GDN_HEADS)
    pairs = range(GDN_PAIRS)
    hsl = [slice(h * GDN_DK, (h + 1) * GDN_DK) for h in heads]

    def blockdiag(x):
        return jnp.concatenate([jnp.where(first, x, 0.0), jnp.where(first, 0.0, x)], axis=0)

    def blockdiag_wide(xa, xb):
        z = jnp.zeros_like(xa)
        return jnp.concatenate([jnp.concatenate([xa, z], axis=1), jnp.concatenate([z, xb], axis=1)], axis=0)

    def rows_of(c):
        return pl.ds(c * CHUNK, CHUNK)

    def head_cols(ref, c, h):
        return ref[rows_of(c), hsl[h]]

    def gcol(c, h):
        return gc_s[rows_of(c), GDN_HEADS + h:GDN_HEADS + h + 1]

    def kbeta(c, h):
        return head_cols(k_s, c, h) * beta_s[rows_of(c), h:h + 1]

    for g0 in range(0, nchunks, GDN_CHUNK_GROUP):
        units = [(c, pr) for c in range(g0, g0 + GDN_CHUNK_GROUP) for pr in pairs]
        decay = [jnp.exp(jnp.where(causal, jnp.where(first, gcol(c, 2 * pr), gcol(c, 2 * pr + 1))
                                   - grow_s[pl.ds(c * GDN_PAIRS + pr, 1), :], -jnp.inf)) for c, pr in units]
        kk = [_dot_nt(jnp.concatenate(
                  [jnp.concatenate([kbeta(c, 2 * pr), kbeta(c, 2 * pr + 1)], axis=1),
                   jnp.concatenate([head_cols(q_s, c, 2 * pr), head_cols(q_s, c, 2 * pr + 1)], axis=1)], axis=0),
                  blockdiag_wide(head_cols(k_s, c, 2 * pr), head_cols(k_s, c, 2 * pr + 1))) for c, pr in units]
        ffn.one()
        nu = range(len(units))
        a = [jnp.where(strict, kk[u][:CHUNK] * decay[u], 0.0) for u in nu]
        qk = [jnp.where(causal, kk[u][CHUNK:] * decay[u], 0.0) for u in nu]
        tinv = [eye - a[u] for u in nu]
        apow = [_dot(a[u], blockdiag(a[u])) for u in nu]
        ffn.one()
        for _ in range(4):
            r = [_dot(jnp.concatenate([apow[u], tinv[u]], axis=0), blockdiag(apow[u])) for u in nu]
            ffn.one()
            apow = [r[u][:CHUNK] for u in nu]
            tinv = [tinv[u] + r[u][CHUNK:] for u in nu]
        tinv = [tinv[u] + _dot(tinv[u], blockdiag(apow[u])) for u in nu]
        ffn.one()

        def rhs(c, h):
            return jnp.concatenate([head_cols(v_s, c, h) * beta_s[rows_of(c), h:h + 1],
                                    kbeta(c, h) * jnp.exp(gcol(c, h))], axis=-1)

        sol_p = [_dot(tinv[u], blockdiag_wide(rhs(c, 2 * pr), rhs(c, 2 * pr + 1)))
                 for u, (c, pr) in enumerate(units)]
        ffn.one()
        width = GDN_DV + GDN_DK
        for ci, c in enumerate(range(g0, g0 + GDN_CHUNK_GROUP)):
            rows = rows_of(c)
            sol = [sol_p[ci * GDN_PAIRS + h // 2][:, (h % 2) * width:(h % 2 + 1) * width] for h in heads]
            s = [state[h] for h in heads]
            ws = [_dot(jnp.concatenate([sol[h][:, GDN_DV:], head_cols(q_s, c, h) * jnp.exp(gcol(c, h))], axis=0),
                       s[h]) for h in heads]
            ffn.one()
            v_new = [sol[h][:, :GDN_DV] - ws[h][:CHUNK] for h in heads]
            for h in heads:
                glast = gc_s[pl.ds(c * CHUNK + CHUNK - 1, 1), GDN_HEADS + h:GDN_HEADS + h + 1]
                k_dec = head_cols(k_s, c, h) * jnp.exp(glast - gcol(c, h))
                state[h] = s[h] * jnp.exp(glast) + _dot_tn(k_dec, v_new[h])
            o_p = [_dot(qk[ci * GDN_PAIRS + pr], blockdiag_wide(v_new[2 * pr], v_new[2 * pr + 1])) for pr in pairs]
            for h in heads:
                o = ws[h][CHUNK:] + o_p[h // 2][:, (h % 2) * GDN_DV:(h % 2 + 1) * GDN_DV]
                z = p_s[pl.ds(SUBLANES + c * CHUNK, CHUNK), GDN_QKV + h * GDN_DV:GDN_QKV + (h + 1) * GDN_DV]
                o_s[cur, rows, hsl[h]] = (_rms(o, ogain_ref[...]) * _silu(z)).astype(BF16)
            ffn.one()
    ffn.rest()
    p_s[0:SUBLANES, :] = p_s[tc:tc + SUBLANES, :]
    p_s[body, :] = pnext_s[...]


def _stacked(rows, cols, index, resident=False):
    mode = dict(pipeline_mode=pl.Buffered(1)) if resident else {}
    return pl.BlockSpec((None, rows, cols), lambda s: (index, 0, 0), **mode)


def _layer_specs(tc, nblk, p_cols, layer, j):
    prev = lambda s: (jnp.maximum(s - 1, 0), 0)
    x_specs = [pl.BlockSpec((tc, D_MODEL), lambda s: (jnp.minimum(s + 1, nblk - 1), 0)),
               pl.BlockSpec((tc, D_MODEL), prev),
               _stacked(1, D_MODEL, layer),
               _stacked(D_MODEL, p_cols, j, resident=True)]
    ffn_specs = [_stacked(D_MODEL, D_MODEL, j, resident=True),
                 _stacked(1, D_MODEL, layer),
                 _stacked(1, D_MODEL, layer),
                 _stacked(D_MODEL, 2 * D_FF, layer, resident=True),
                 _stacked(FFN_CONV, 2 * D_FF, layer),
                 _stacked(D_FF, D_MODEL, layer, resident=True),
                 _stacked(1, D_MODEL, layer)]
    y_spec = pl.BlockSpec((tc, D_MODEL), prev)
    return x_specs, ffn_specs, y_spec


def _gdn_layer(x2, g_in, w_in, conv_w, alog_rows, dtb_rows, o_gain, ffn_args, seq, layer, j, tc=SEQ_BLOCK):
    t = x2.shape[0]
    nb = seq // tc
    nblk = t // tc
    x_specs, ffn_specs, y_spec = _layer_specs(tc, nblk, GDN_IN_PAD, layer, j)
    return pl.pallas_call(
        functools.partial(_gdn_layer_kernel, tc=tc, nb=nb),
        grid=(nblk + 1,),
        in_specs=x_specs + [
                  _stacked(GDN_CONV, GDN_QKV, j),
                  _stacked(1, LANES, j),
                  _stacked(1, LANES, j),
                  _stacked(1, GDN_DV, j)] + ffn_specs,
        out_specs=y_spec,
        out_shape=jax.ShapeDtypeStruct((t, D_MODEL), F32),
        scratch_shapes=[pltpu.VMEM((tc + SUBLANES, GDN_IN_PAD), F32),
                        pltpu.VMEM((tc, GDN_IN_PAD), F32),
                        pltpu.VMEM((tc, D_MODEL), BF16),
                        pltpu.VMEM((tc, GDN_QK), F32),
                        pltpu.VMEM((tc, GDN_QK), F32),
                        pltpu.VMEM((tc, GDN_V), F32),
                        pltpu.VMEM((tc, LANES), F32),
                        pltpu.VMEM((tc, LANES), F32),
                        pltpu.VMEM((tc // CHUNK * GDN_PAIRS, LANES), F32),
                        pltpu.VMEM((GDN_HEADS, GDN_DK, GDN_DV), F32),
                        ] + _FFN_SCRATCH(tc),
        compiler_params=_params("arbitrary"),
        name="gdn_layer",
    )(x2, x2, g_in, w_in, conv_w, alog_rows, dtb_rows, o_gain, *ffn_args)


def _gla_layer_kernel(xn_ref, x_ref, gin_ref, win_ref, wgu_ref, gbias_ref, ogain_ref,
                      wout_ref, gmix_ref, gpre_ref, wup_ref, fconvw_ref, wdown_ref, gffn_ref,
                      y_ref,
                      p_ref, pnext_s, hin_s, bc_s, state,
                      o_s, m_s, xmid_s, h_s, ubuf, fhalo, act_s, f_s, *, tc, nb):
    nchunks = tc // CHUNK
    s_id = pl.program_id(0)
    cur = s_id % 2

    @pl.when(s_id % nb == 0)
    def _():
        state[...] = jnp.zeros_like(state)

    _init_skew(s_id, nb, o_s, fhalo)
    _first_inproj(s_id, x_ref, gin_ref, win_ref, p_ref, p_ref)
    ffn = _Interleaver(
        _spread(_ffn_steps(o_s.at[1 - cur], x_ref, wout_ref, gmix_ref, gpre_ref, wup_ref,
                           fconvw_ref, wdown_ref, gffn_ref, y_ref,
                           m_s, xmid_s, h_s, ubuf, fhalo, act_s, f_s, tc),
                _inproj_steps(xn_ref, gin_ref, win_ref, hin_s, pnext_s, GLA_IN_PAD, MXU_DIM)),
        slots=2 + GLA_SLOTS_PER_CHUNK * nchunks)

    z = _dot(p_ref[:, GLA_MAIN:GLA_MAIN + LANES], wgu_ref[...]) + gbias_ref[...]
    log_a = -_softplus(-z) * (1.0 / GLA_TAU)
    ffn.one()
    bc_s[...] = _chunk_cumsum(log_a)
    ffn.one()

    ii = lax.broadcasted_iota(jnp.int32, (CHUNK, CHUNK), 0)
    jj = lax.broadcasted_iota(jnp.int32, (CHUNK, CHUNK), 1)
    level_masks = [((ii // lv) == (jj // lv) + 1) & (((ii // lv) % 2) == 1) for lv in GLA_LEVELS]
    isub = lax.broadcasted_iota(jnp.int32, (GLA_SUB, GLA_DK), 0)
    jsub = lax.broadcasted_iota(jnp.int32, (GLA_SUB, CHUNK), 1)
    scale = GLA_DK ** -0.5
    heads = range(GLA_HEADS)

    def block_rows(b, lv, shift):
        parts = []
        for m in range(CHUNK // lv):
            r = min((m + shift) * lv, CHUNK - 1)
            parts.append(jnp.broadcast_to(b[r:r + 1], (lv, GLA_DK)))
        return jnp.concatenate(parts, axis=0)

    for c in range(nchunks):
        rows = pl.ds(c * CHUNK, CHUNK)
        q = [p_ref[rows, h * GLA_DK:(h + 1) * GLA_DK] * scale for h in heads]
        k = [p_ref[rows, GLA_K + h * GLA_DK:GLA_K + (h + 1) * GLA_DK] for h in heads]
        b = [bc_s[rows, h * GLA_DK:(h + 1) * GLA_DK] for h in heads]
        attn = [jnp.zeros((CHUNK, CHUNK), F32) for h in heads]
        for lv, mask in zip(GLA_LEVELS, level_masks):
            q_t = [q[h] * jnp.exp(b[h] - block_rows(b[h], lv, 0)) for h in heads]
            k_t = [k[h] * jnp.exp(block_rows(b[h], lv, 1) - b[h]) for h in heads]
            attn = [jnp.where(mask, _dot_nt(q_t[h], k_t[h]), attn[h]) for h in heads]
            ffn.one()
        diag = []
        for h in heads:
            blocks = []
            for blk in range(CHUNK // GLA_SUB):
                i0 = blk * GLA_SUB
                q_i = q[h][i0:i0 + GLA_SUB]
                b_i = b[h][i0:i0 + GLA_SUB]
                k_i = k[h][i0:i0 + GLA_SUB]
                acc = jnp.zeros((GLA_SUB, CHUNK), F32)
                for j in range(GLA_SUB):
                    e = jnp.exp(jnp.where(isub >= j, b_i - b_i[j:j + 1], -jnp.inf))
                    col = jnp.sum(q_i * k_i[j:j + 1] * e, axis=-1, keepdims=True)
                    acc = jnp.where(jsub == i0 + j, col, acc)
                blocks.append(acc)
            diag.append(jnp.concatenate(blocks, axis=0))
            if h % 2 == 1:
                ffn.one()
        st = [state[h] for h in heads]
        o = []
        for h in heads:
            v = p_ref[rows, 2 * GLA_K + h * GLA_DV:2 * GLA_K + (h + 1) * GLA_DV]
            blast = b[h][CHUNK - 1:CHUNK, :]
            o.append(_dot_nt(q[h] * jnp.exp(b[h]), st[h]) + _dot(attn[h] + diag[h], v))
            state[h] = st[h] * jnp.exp(blast) + _dot_tn(v, k[h] * jnp.exp(blast - b[h]))
        ffn.one()
        for h in heads:
            r = p_ref[rows, 2 * GLA_K + GLA_V + h * GLA_DV:2 * GLA_K + GLA_V + (h + 1) * GLA_DV]
            o_s[cur, rows, h * GLA_DV:(h + 1) * GLA_DV] = (
                _rms(o[h], ogain_ref[...]) * _silu(r)).astype(BF16)
    ffn.rest()
    p_ref[...] = pnext_s[...]


def _gla_layer(x2, g_in, w_in, wgu, gate_bias, o_gain, ffn_args, seq, layer, j, tc=SEQ_BLOCK):
    t = x2.shape[0]
    nb = seq // tc
    nblk = t // tc
    x_specs, ffn_specs, y_spec = _layer_specs(tc, nblk, GLA_IN_PAD, layer, j)
    return pl.pallas_call(
        functools.partial(_gla_layer_kernel, tc=tc, nb=nb),
        grid=(nblk + 1,),
        in_specs=x_specs + [
                  _stacked(LANES, GLA_K, j),
                  _stacked(1, GLA_K, j),
                  _stacked(1, GLA_DV, j)] + ffn_specs,
        out_specs=y_spec,
        out_shape=jax.ShapeDtypeStruct((t, D_MODEL), F32),
        scratch_shapes=[pltpu.VMEM((tc, GLA_IN_PAD), F32),
                        pltpu.VMEM((tc, GLA_IN_PAD), F32),
                        pltpu.VMEM((tc, D_MODEL), BF16),
                        pltpu.VMEM((tc, GLA_K), F32),
                        pltpu.VMEM((GLA_HEADS, GLA_DV, GLA_DK), F32),
                        ] + _FFN_SCRATCH(tc),
        compiler_params=_params("arbitrary"),
        name="gla_layer",
    )(x2, x2, g_in, w_in, wgu, gate_bias, o_gain, *ffn_args)


def _pad_last(w, n):
    return jnp.pad(w, [(0, 0)] * (w.ndim - 1) + [(0, n - w.shape[-1])])


def _rows(v):
    return v[:, None, :]


def kernel(x, gdn_w_in, gdn_conv_w, gdn_a_log, gdn_dt_bias, gdn_o_norm, gdn_w_out,
           gla_w_in, gla_w_gate_up, gla_gate_bias, gla_o_norm, gla_w_out,
           mix_pre_norm, mix_post_norm, ffn_pre_norm, ffn_post_norm,
           ffn_w_up, ffn_conv_w, ffn_w_down):
    batch, seq, d = x.shape
    depth = mix_pre_norm.shape[0]
    x2 = x.reshape(batch * seq, d)
    g_pre = _rows(mix_pre_norm)
    ffn_tail = (_rows(mix_post_norm), _rows(ffn_pre_norm), ffn_w_up.astype(BF16), ffn_conv_w,
                ffn_w_down.astype(BF16), _rows(ffn_post_norm))
    gdn_w_in_p = _pad_last(gdn_w_in, GDN_IN_PAD).astype(BF16)
    gla_w_in_p = _pad_last(gla_w_in, GLA_IN_PAD).astype(BF16)
    gdn_w_out_b = gdn_w_out.astype(BF16)
    gla_w_out_b = gla_w_out.astype(BF16)
    on_decay_lanes = lambda v: _rows(jnp.pad(v, ((0, 0), (GDN_HEADS, LANES - 2 * GDN_HEADS))))
    alog_rows = on_decay_lanes(gdn_a_log)
    dtb_rows = on_decay_lanes(gdn_dt_bias)
    wgu = jnp.pad(gla_w_gate_up, ((0, 0), (0, LANES - GLA_GATE_RANK), (0, 0))).astype(BF16)
    for layer in range(depth):
        j = layer // 2
        if layer % 2 == 0:
            x2 = _gdn_layer(x2, g_pre, gdn_w_in_p, gdn_conv_w, alog_rows, dtb_rows, _rows(gdn_o_norm),
                            (gdn_w_out_b,) + ffn_tail, seq, layer, j)
        else:
            x2 = _gla_layer(x2, g_pre, gla_w_in_p, wgu, _rows(gla_gate_bias), _rows(gla_o_norm),
                            (gla_w_out_b,) + ffn_tail, seq, layer, j)
    return x2.reshape(batch, seq, d)
```

```python
import functools

import jax
import jax.numpy as jnp
from jax import lax
from jax.experimental import pallas as pl
from jax.experimental.pallas import tpu as pltpu

EPS = 1e-6
CHUNK = 64
LANES = 128
SUBLANES = 8
MXU_DIM = 256
D_MODEL = 1024

GDN_HEADS = 8
GDN_DK = 128
GDN_DV = 128
GDN_CONV = 4
GDN_QK = GDN_HEADS * GDN_DK
GDN_V = GDN_HEADS * GDN_DV
GDN_PAIRS = GDN_HEADS // 2
GDN_QKV = 2 * GDN_QK + GDN_V
GDN_MAIN = GDN_QKV + GDN_V
GDN_IN_PAD = GDN_MAIN + LANES

GLA_HEADS = 4
GLA_DK = 128
GLA_DV = 256
GLA_GATE_RANK = 16
GLA_TAU = 16.0
GLA_K = GLA_HEADS * GLA_DK
GLA_V = GLA_HEADS * GLA_DV
GLA_MAIN = 2 * GLA_K + 2 * GLA_V
GLA_IN_PAD = GLA_MAIN + LANES
GLA_SUB = 8
GLA_LEVELS = (32, 16, 8)

D_FF = 2816
FFN_CONV = 3
FFN_TILE = MXU_DIM
SEQ_BLOCK = 256
GDN_CHUNK_GROUP = 4
GDN_GROUP_SLOTS = 8
GLA_SLOTS_PER_CHUNK = 6

VMEM_LIMIT = 56 * 1024 * 1024

BF16 = jnp.bfloat16
F32 = jnp.float32


def _dot(a, b):
    return jnp.dot(a.astype(BF16), b.astype(BF16), preferred_element_type=F32)


def _dot_nt(a, b):
    return lax.dot_general(a.astype(BF16), b.astype(BF16), (((1,), (1,)), ((), ())),
                           preferred_element_type=F32)


def _dot_tn(a, b):
    return lax.dot_general(a.astype(BF16), b.astype(BF16), (((0,), (0,)), ((), ())),
                           preferred_element_type=F32)


def _rms(x, g):
    return x * lax.rsqrt(jnp.mean(x * x, axis=-1, keepdims=True) + EPS) * g


def _silu(x):
    return x * jax.nn.sigmoid(x)


def _softplus(x):
    return jnp.maximum(x, 0.0) + jnp.log1p(jnp.exp(-jnp.abs(x)))


def _chunk_cumsum(x):
    n = x.shape[0]
    i = lax.broadcasted_iota(jnp.int32, (n, n), 0)
    j = lax.broadcasted_iota(jnp.int32, (n, n), 1)
    tri = jnp.where((j <= i) & ((i // CHUNK) == (j // CHUNK)), 1.0, 0.0).astype(BF16)
    hi = x.astype(BF16)
    rest = x - hi.astype(F32)
    mid = rest.astype(BF16)
    lo = (rest - mid.astype(F32)).astype(BF16)
    dot = lambda t: jnp.dot(tri, t, preferred_element_type=F32)
    return dot(hi) + dot(mid) + dot(lo)


def _params(*sem):
    return pltpu.CompilerParams(dimension_semantics=sem, vmem_limit_bytes=VMEM_LIMIT)


_FFN_SCRATCH = lambda tm: [
    pltpu.VMEM((2, tm, D_MODEL), BF16),
    pltpu.VMEM((tm, D_MODEL), F32),
    pltpu.VMEM((tm, D_MODEL), F32),
    pltpu.VMEM((tm, D_MODEL), BF16),
    pltpu.VMEM((2, tm + SUBLANES, FFN_TILE), F32),
    pltpu.VMEM((SUBLANES, 2 * D_FF), F32),
    pltpu.VMEM((tm, D_FF), BF16),
    pltpu.VMEM((tm, D_MODEL), F32),
]


def _ffn_steps(o_prev, x_ref, wout_ref, gmix_ref, gpre_ref, wup_ref, fconvw_ref, wdown_ref, gffn_ref,
               y_ref, m_s, xmid_s, h_s, ubuf, fhalo, act_s, f_s, tm):
    steps = []

    def out_proj(n):
        cs = slice(n * MXU_DIM, (n + 1) * MXU_DIM)
        m_s[:, cs] = jnp.dot(o_prev[...], wout_ref[:, cs], preferred_element_type=F32)

    def norms():
        xm = x_ref[...] + _rms(m_s[...], gmix_ref[...])
        xmid_s[...] = xm
        h_s[...] = _rms(xm, gpre_ref[...]).astype(BF16)

    def conv_tile(col, buf):
        u = jnp.dot(h_s[...], wup_ref[:, col:col + FFN_TILE], preferred_element_type=F32)
        ubuf[buf, 0:SUBLANES, :] = fhalo[:, col:col + FFN_TILE]
        ubuf[buf, SUBLANES:SUBLANES + tm, :] = u
        fhalo[:, col:col + FFN_TILE] = u[tm - SUBLANES:tm]
        acc = fconvw_ref[FFN_CONV - 1:FFN_CONV, col:col + FFN_TILE] * u
        for kk in range(FFN_CONV - 1):
            r0 = SUBLANES - (FFN_CONV - 1) + kk
            acc = acc + fconvw_ref[kk:kk + 1, col:col + FFN_TILE] * ubuf[buf, r0:r0 + tm, :]
        return acc

    def up(j):
        gate = conv_tile(j * FFN_TILE, 0)
        val = conv_tile(D_FF + j * FFN_TILE, 1)
        act_s[:, j * FFN_TILE:(j + 1) * FFN_TILE] = (_silu(gate) * val).astype(BF16)

    def down(n):
        cs = slice(n * MXU_DIM, (n + 1) * MXU_DIM)
        f_s[:, cs] = jnp.dot(act_s[...], wdown_ref[:, cs], preferred_element_type=F32)

    def finish():
        y_ref[...] = xmid_s[...] + _rms(f_s[...], gffn_ref[...])

    for n in range(D_MODEL // MXU_DIM):
        steps.append(functools.partial(out_proj, n))
    steps.append(norms)
    for j in range(D_FF // FFN_TILE):
        steps.append(functools.partial(up, j))
    for n in range(D_MODEL // MXU_DIM):
        steps.append(functools.partial(down, n))
    steps.append(finish)
    return steps


class _Interleaver:
    def __init__(self, steps, slots):
        self._steps = list(steps)
        self._total = len(self._steps)
        self._slots = slots
        self._calls = 0

    def one(self):
        self._calls += 1
        due = -(-self._calls * self._total // self._slots)
        while self._steps and self._total - len(self._steps) < due:
            self._steps.pop(0)()

    def rest(self):
        while self._steps:
            self._steps.pop(0)()


def _init_skew(s, nb, o_s, fhalo):
    @pl.when(s == 0)
    def _():
        o_s[...] = jnp.zeros_like(o_s)

    @pl.when((s == 0) | ((s + nb - 1) % nb == 0))
    def _():
        fhalo[...] = jnp.zeros_like(fhalo)


def _inproj_steps(xn_ref, gin_ref, win_ref, hin_s, dst, n_cols, tile):
    def norm():
        hin_s[...] = _rms(xn_ref[...], gin_ref[...]).astype(BF16)

    def tile_step(c0, c1):
        dst[:, c0:c1] = jnp.dot(hin_s[...], win_ref[:, c0:c1], preferred_element_type=F32)

    steps = [norm]
    for c0 in range(0, n_cols, tile):
        steps.append(functools.partial(tile_step, c0, min(c0 + tile, n_cols)))
    return steps


def _spread(steps, extra):
    out, done = [], 0
    for i, step in enumerate(steps):
        out.append(step)
        due = (i + 1) * len(extra) // len(steps)
        out.extend(extra[done:due])
        done = due
    return out


def _first_inproj(s, x_ref, gin_ref, win_ref, p_s, dst0):
    @pl.when(s == 0)
    def _():
        p_s[...] = jnp.zeros_like(p_s)
        dst0[...] = _dot(_rms(x_ref[...], gin_ref[...]), win_ref[...])


def _gdn_layer_kernel(xn_ref, x_ref, gin_ref, win_ref, convw_ref, alog_ref, dtb_ref, ogain_ref,
                      wout_ref, gmix_ref, gpre_ref, wup_ref, fconvw_ref, wdown_ref, gffn_ref,
                      y_ref,
                      p_s, pnext_s, hin_s, q_s, k_s, v_s, beta_s, gc_s, grow_s, state,
                      o_s, m_s, xmid_s, h_s, ubuf, fhalo, act_s, f_s, *, tc, nb):
    nchunks = tc // CHUNK
    s_id = pl.program_id(0)
    cur = s_id % 2
    body = pl.ds(SUBLANES, tc)

    _init_skew(s_id, nb, o_s, fhalo)
    _first_inproj(s_id, x_ref, gin_ref, win_ref, p_s, p_s.at[body, :])

    @pl.when(s_id % nb == 0)
    def _():
        state[...] = jnp.zeros_like(state)
        p_s[0:SUBLANES, :] = jnp.zeros((SUBLANES, GDN_IN_PAD), F32)

    ffn = _Interleaver(_ffn_steps(o_s.at[1 - cur], x_ref, wout_ref, gmix_ref, gpre_ref, wup_ref,
                                  fconvw_ref, wdown_ref, gffn_ref, y_ref,
                                  m_s, xmid_s, h_s, ubuf, fhalo, act_s, f_s, tc),
                       slots=GDN_GROUP_SLOTS * (nchunks // GDN_CHUNK_GROUP))

    def conv_silu(col):
        acc = convw_ref[GDN_CONV - 1:GDN_CONV, col:col + LANES] * p_s[body, col:col + LANES]
        for kk in range(GDN_CONV - 1):
            r0 = SUBLANES - (GDN_CONV - 1) + kk
            acc = acc + convw_ref[kk:kk + 1, col:col + LANES] * p_s[pl.ds(r0, tc), col:col + LANES]
        return _silu(acc)

    def l2n(t):
        return t * lax.rsqrt(jnp.sum(t * t, axis=-1, keepdims=True) + EPS)

    for h in range(GDN_HEADS):
        c0 = h * GDN_DK
        q_s[:, c0:c0 + GDN_DK] = l2n(conv_silu(c0)) * (GDN_DK ** -0.5)
        k_s[:, c0:c0 + GDN_DK] = l2n(conv_silu(GDN_QK + c0))
        v_s[:, c0:c0 + GDN_DV] = conv_silu(2 * GDN_QK + h * GDN_DV)
    for step in _inproj_steps(xn_ref, gin_ref, win_ref, hin_s, pnext_s, GDN_IN_PAD, MXU_DIM):
        step()

    pg = p_s[body, GDN_MAIN:GDN_MAIN + LANES]
    lane = lax.broadcasted_iota(jnp.int32, (tc, LANES), 1)
    beta_s[...] = jax.nn.sigmoid(pg)
    g = -jnp.exp(alog_ref[...]) * _softplus(pg + dtb_ref[...])
    g = jnp.where((lane >= GDN_HEADS) & (lane < 2 * GDN_HEADS), g, 0.0)
    gc = _chunk_cumsum(g)
    gc_s[...] = gc
    gct = gc.T
    lane_row = lax.broadcasted_iota(jnp.int32, (1, LANES), 1)
    for m in range(tc // LANES):
        tile = gct[GDN_HEADS:2 * GDN_HEADS, m * LANES:(m + 1) * LANES]
        swapped = pltpu.roll(tile, CHUNK, axis=1)
        for pr in range(GDN_PAIRS):
            a, b = 2 * pr, 2 * pr + 1
            grow_s[pl.ds((2 * m) * GDN_PAIRS + pr, 1), :] = jnp.where(
                lane_row < CHUNK, tile[a:a + 1], swapped[b:b + 1])
            grow_s[pl.ds((2 * m + 1) * GDN_PAIRS + pr, 1), :] = jnp.where(
                lane_row < CHUNK, swapped[a:a + 1], tile[b:b + 1])

    ii = lax.broadcasted_iota(jnp.int32, (CHUNK, 2 * CHUNK), 0)
    ll = lax.broadcasted_iota(jnp.int32, (CHUNK, 2 * CHUNK), 1)
    first = ll < CHUNK
    jj = jnp.where(first, ll, ll - CHUNK)
    causal = ii >= jj
    strict = ii > jj
    eye = jnp.where(ii == jj, 1.0, 0.0).astype(F32)
    heads = range(GDN_HEADS)
    pairs = range(GDN_PAIRS)
    hsl = [slice(h * GDN_DK, (h + 1) * GDN_DK) for h in heads]

    def blockdiag(x):
        return jnp.concatenate([jnp.where(first, x, 0.0), jnp.where(first, 0.0, x)], axis=0)

    def blockdiag_wide(xa, xb):
        z = jnp.zeros_like(xa)
        return jnp.concatenate([jnp.concatenate([xa, z], axis=1), jnp.concatenate([z, xb], axis=1)], axis=0)

    def rows_of(c):
        return pl.ds(c * CHUNK, CHUNK)

    def head_cols(ref, c, h):
        return ref[rows_of(c), hsl[h]]

    def gcol(c, h):
        return gc_s[rows_of(c), GDN_HEADS + h:GDN_HEADS + h + 1]

    def kbeta(c, h):
        return head_cols(k_s, c, h) * beta_s[rows_of(c), h:h + 1]

    for g0 in range(0, nchunks, GDN_CHUNK_GROUP):
        units = [(c, pr) for c in range(g0, g0 + GDN_CHUNK_GROUP) for pr in pairs]
        decay = [jnp.exp(jnp.where(causal, jnp.where(first, gcol(c, 2 * pr), gcol(c, 2 * pr + 1))
                                   - grow_s[pl.ds(c * GDN_PAIRS + pr, 1), :], -jnp.inf)) for c, pr in units]
        kk = [_dot_nt(jnp.concatenate(
                  [jnp.concatenate([kbeta(c, 2 * pr), kbeta(c, 2 * pr + 1)], axis=1),
                   jnp.concatenate([head_cols(q_s, c, 2 * pr), head_cols(q_s, c, 2 * pr + 1)], axis=1)], axis=0),
                  blockdiag_wide(head_cols(k_s, c, 2 * pr), head_cols(k_s, c, 2 * pr + 1))) for c, pr in units]
        ffn.one()
        nu = range(len(units))
        a = [jnp.where(strict, kk[u][:CHUNK] * decay[u], 0.0) for u in nu]
        qk = [jnp.where(causal, kk[u][CHUNK:] * decay[u], 0.0) for u in nu]
        tinv = [eye - a[u] for u in nu]
        apow = [_dot(a[u], blockdiag(a[u])) for u in nu]
        ffn.one()
        for _ in range(4):
            r = [_dot(jnp.concatenate([apow[u], tinv[u]], axis=0), blockdiag(apow[u])) for u in nu]
            ffn.one()
            apow = [r[u][:CHUNK] for u in nu]
            tinv = [tinv[u] + r[u][CHUNK:] for u in nu]
        tinv = [tinv[u] + _dot(tinv[u], blockdiag(apow[u])) for u in nu]
        ffn.one()

        def rhs(c, h):
            return jnp.concatenate([head_cols(v_s, c, h) * beta_s[rows_of(c), h:h + 1],
                                    kbeta(c, h) * jnp.exp(gcol(c, h))], axis=-1)

        sol_p = [_dot(tinv[u], blockdiag_wide(rhs(c, 2 * pr), rhs(c, 2 * pr + 1)))
                 for u, (c, pr) in enumerate(units)]
        ffn.one()
        width = GDN_DV + GDN_DK
        for ci, c in enumerate(range(g0, g0 + GDN_CHUNK_GROUP)):
            rows = rows_of(c)
            sol = [sol_p[ci * GDN_PAIRS + h // 2][:, (h % 2) * width:(h % 2 + 1) * width] for h in heads]
            s = [state[h] for h in heads]
            ws = [_dot(jnp.concatenate([sol[h][:, GDN_DV:], head_cols(q_s, c, h) * jnp.exp(gcol(c, h))], axis=0),
                       s[h]) for h in heads]
            v_new = [sol[h][:, :GDN_DV] - ws[h][:CHUNK] for h in heads]
            for h in heads:
                glast = gc_s[pl.ds(c * CHUNK + CHUNK - 1, 1), GDN_HEADS + h:GDN_HEADS + h + 1]
                k_dec = head_cols(k_s, c, h) * jnp.exp(glast - gcol(c, h))
                state[h] = s[h] * jnp.exp(glast) + _dot_tn(k_dec, v_new[h])
            o_p = [_dot(qk[ci * GDN_PAIRS + pr], blockdiag_wide(v_new[2 * pr], v_new[2 * pr + 1])) for pr in pairs]
            for h in heads:
                o = ws[h][CHUNK:] + o_p[h // 2][:, (h % 2) * GDN_DV:(h % 2 + 1) * GDN_DV]
                z = p_s[pl.ds(SUBLANES + c * CHUNK, CHUNK), GDN_QKV + h * GDN_DV:GDN_QKV + (h + 1) * GDN_DV]
                o_s[cur, rows, hsl[h]] = (_rms(o, ogain_ref[...]) * _silu(z)).astype(BF16)
    ffn.rest()
    p_s[0:SUBLANES, :] = p_s[tc:tc + SUBLANES, :]
    p_s[body, :] = pnext_s[...]


def _stacked(rows, cols, index, resident=False):
    mode = dict(pipeline_mode=pl.Buffered(1)) if resident else {}
    return pl.BlockSpec((None, rows, cols), lambda s: (index, 0, 0), **mode)


def _layer_specs(tc, nblk, p_cols, layer, j):
    prev = lambda s: (jnp.maximum(s - 1, 0), 0)
    x_specs = [pl.BlockSpec((tc, D_MODEL), lambda s: (jnp.minimum(s + 1, nblk - 1), 0)),
               pl.BlockSpec((tc, D_MODEL), prev),
               _stacked(1, D_MODEL, layer),
               _stacked(D_MODEL, p_cols, j, resident=True)]
    ffn_specs = [_stacked(D_MODEL, D_MODEL, j, resident=True),
                 _stacked(1, D_MODEL, layer),
                 _stacked(1, D_MODEL, layer),
                 _stacked(D_MODEL, 2 * D_FF, layer, resident=True),
                 _stacked(FFN_CONV, 2 * D_FF, layer),
                 _stacked(D_FF, D_MODEL, layer, resident=True),
                 _stacked(1, D_MODEL, layer)]
    y_spec = pl.BlockSpec((tc, D_MODEL), prev)
    return x_specs, ffn_specs, y_spec


def _gdn_layer(x2, g_in, w_in, conv_w, alog_rows, dtb_rows, o_gain, ffn_args, seq, layer, j, tc=SEQ_BLOCK):
    t = x2.shape[0]
    nb = seq // tc
    nblk = t // tc
    x_specs, ffn_specs, y_spec = _layer_specs(tc, nblk, GDN_IN_PAD, layer, j)
    return pl.pallas_call(
        functools.partial(_gdn_layer_kernel, tc=tc, nb=nb),
        grid=(nblk + 1,),
        in_specs=x_specs + [
                  _stacked(GDN_CONV, GDN_QKV, j),
                  _stacked(1, LANES, j),
                  _stacked(1, LANES, j),
                  _stacked(1, GDN_DV, j)] + ffn_specs,
        out_specs=y_spec,
        out_shape=jax.ShapeDtypeStruct((t, D_MODEL), F32),
        scratch_shapes=[pltpu.VMEM((tc + SUBLANES, GDN_IN_PAD), F32),
                        pltpu.VMEM((tc, GDN_IN_PAD), F32),
                        pltpu.VMEM((tc, D_MODEL), BF16),
                        pltpu.VMEM((tc, GDN_QK), F32),
                        pltpu.VMEM((tc, GDN_QK), F32),
                        pltpu.VMEM((tc, GDN_V), F32),
                        pltpu.VMEM((tc, LANES), F32),
                        pltpu.VMEM((tc, LANES), F32),
                        pltpu.VMEM((tc // CHUNK * GDN_PAIRS, LANES), F32),
                        pltpu.VMEM((GDN_HEADS, GDN_DK, GDN_DV), F32),
                        ] + _FFN_SCRATCH(tc),
        compiler_params=_params("arbitrary"),
        name="gdn_layer",
    )(x2, x2, g_in, w_in, conv_w, alog_rows, dtb_rows, o_gain, *ffn_args)


def _gla_layer_kernel(xn_ref, x_ref, gin_ref, win_ref, wgu_ref, gbias_ref, ogain_ref,
                      wout_ref, gmix_ref, gpre_ref, wup_ref, fconvw_ref, wdown_ref, gffn_ref,
                      y_ref,
                      p_ref, pnext_s, hin_s, bc_s, state,
                      o_s, m_s, xmid_s, h_s, ubuf, fhalo, act_s, f_s, *, tc, nb):
    nchunks = tc // CHUNK
    s_id = pl.program_id(0)
    cur = s_id % 2

    @pl.when(s_id % nb == 0)
    def _():
        state[...] = jnp.zeros_like(state)

    _init_skew(s_id, nb, o_s, fhalo)
    _first_inproj(s_id, x_ref, gin_ref, win_ref, p_ref, p_ref)
    ffn = _Interleaver(
        _spread(_ffn_steps(o_s.at[1 - cur], x_ref, wout_ref, gmix_ref, gpre_ref, wup_ref,
                           fconvw_ref, wdown_ref, gffn_ref, y_ref,
                           m_s, xmid_s, h_s, ubuf, fhalo, act_s, f_s, tc),
                _inproj_steps(xn_ref, gin_ref, win_ref, hin_s, pnext_s, GLA_IN_PAD, MXU_DIM)),
        slots=2 + GLA_SLOTS_PER_CHUNK * nchunks)

    z = _dot(p_ref[:, GLA_MAIN:GLA_MAIN + LANES], wgu_ref[...]) + gbias_ref[...]
    log_a = -_softplus(-z) * (1.0 / GLA_TAU)
    ffn.one()
    bc_s[...] = _chunk_cumsum(log_a)
    ffn.one()

    ii = lax.broadcasted_iota(jnp.int32, (CHUNK, CHUNK), 0)
    jj = lax.broadcasted_iota(jnp.int32, (CHUNK, CHUNK), 1)
    level_masks = [((ii // lv) == (jj // lv) + 1) & (((ii // lv) % 2) == 1) for lv in GLA_LEVELS]
    isub = lax.broadcasted_iota(jnp.int32, (GLA_SUB, GLA_DK), 0)
    jsub = lax.broadcasted_iota(jnp.int32, (GLA_SUB, CHUNK), 1)
    scale = GLA_DK ** -0.5
    heads = range(GLA_HEADS)

    def block_rows(b, lv, shift):
        parts = []
        for m in range(CHUNK // lv):
            r = min((m + shift) * lv, CHUNK - 1)
            parts.append(jnp.broadcast_to(b[r:r + 1], (lv, GLA_DK)))
        return jnp.concatenate(parts, axis=0)

    for c in range(nchunks):
        rows = pl.ds(c * CHUNK, CHUNK)
        q = [p_ref[rows, h * GLA_DK:(h + 1) * GLA_DK] * scale for h in heads]
        k = [p_ref[rows, GLA_K + h * GLA_DK:GLA_K + (h + 1) * GLA_DK] for h in heads]
        b = [bc_s[rows, h * GLA_DK:(h + 1) * GLA_DK] for h in heads]
        attn = [jnp.zeros((CHUNK, CHUNK), F32) for h in heads]
        for lv, mask in zip(GLA_LEVELS, level_masks):
            q_t = [q[h] * jnp.exp(b[h] - block_rows(b[h], lv, 0)) for h in heads]
            k_t = [k[h] * jnp.exp(block_rows(b[h], lv, 1) - b[h]) for h in heads]
            attn = [jnp.where(mask, _dot_nt(q_t[h], k_t[h]), attn[h]) for h in heads]
            ffn.one()
        diag = []
        for h in heads:
            blocks = []
            for blk in range(CHUNK // GLA_SUB):
                i0 = blk * GLA_SUB
                q_i = q[h][i0:i0 + GLA_SUB]
                b_i = b[h][i0:i0 + GLA_SUB]
                k_i = k[h][i0:i0 + GLA_SUB]
                acc = jnp.zeros((GLA_SUB, CHUNK), F32)
                for j in range(GLA_SUB):
                    e = jnp.exp(jnp.where(isub >= j, b_i - b_i[j:j + 1], -jnp.inf))
                    col = jnp.sum(q_i * k_i[j:j + 1] * e, axis=-1, keepdims=True)
                    acc = jnp.where(jsub == i0 + j, col, acc)
                blocks.append(acc)
            diag.append(jnp.concatenate(blocks, axis=0))
            if h % 2 == 1:
                ffn.one()
        st = [state[h] for h in heads]
        o = []
        for h in heads:
            v = p_ref[rows, 2 * GLA_K + h * GLA_DV:2 * GLA_K + (h + 1) * GLA_DV]
            blast = b[h][CHUNK - 1:CHUNK, :]
            o.append(_dot(jnp.concatenate([q[h] * jnp.exp(b[h]), attn[h] + diag[h]], axis=1),
                          jnp.concatenate([st[h], v], axis=0)))
            decay_col = jnp.transpose(jnp.broadcast_to(jnp.exp(blast), (SUBLANES, GLA_DK)))[:, 0:1]
            state[h] = st[h] * decay_col + _dot_tn(k[h] * jnp.exp(blast - b[h]), v)
        ffn.one()
        for h in heads:
            r = p_ref[rows, 2 * GLA_K + GLA_V + h * GLA_DV:2 * GLA_K + GLA_V + (h + 1) * GLA_DV]
            o_s[cur, rows, h * GLA_DV:(h + 1) * GLA_DV] = (
                _rms(o[h], ogain_ref[...]) * _silu(r)).astype(BF16)
    ffn.rest()
    p_ref[...] = pnext_s[...]


def _gla_layer(x2, g_in, w_in, wgu, gate_bias, o_gain, ffn_args, seq, layer, j, tc=SEQ_BLOCK):
    t = x2.shape[0]
    nb = seq // tc
    nblk = t // tc
    x_specs, ffn_specs, y_spec = _layer_specs(tc, nblk, GLA_IN_PAD, layer, j)
    return pl.pallas_call(
        functools.partial(_gla_layer_kernel, tc=tc, nb=nb),
        grid=(nblk + 1,),
        in_specs=x_specs + [
                  _stacked(LANES, GLA_K, j),
                  _stacked(1, GLA_K, j),
                  _stacked(1, GLA_DV, j)] + ffn_specs,
        out_specs=y_spec,
        out_shape=jax.ShapeDtypeStruct((t, D_MODEL), F32),
        scratch_shapes=[pltpu.VMEM((tc, GLA_IN_PAD), F32),
                        pltpu.VMEM((tc, GLA_IN_PAD), F32),
                        pltpu.VMEM((tc, D_MODEL), BF16),
                        pltpu.VMEM((tc, GLA_K), F32),
                        pltpu.VMEM((GLA_HEADS, GLA_DK, GLA_DV), F32),
                        ] + _FFN_SCRATCH(tc),
        compiler_params=_params("arbitrary"),
        name="gla_layer",
    )(x2, x2, g_in, w_in, wgu, gate_bias, o_gain, *ffn_args)


def _pad_last(w, n):
    return jnp.pad(w, [(0, 0)] * (w.ndim - 1) + [(0, n - w.shape[-1])])


def _rows(v):
    return v[:, None, :]


def kernel(x, gdn_w_in, gdn_conv_w, gdn_a_log, gdn_dt_bias, gdn_o_norm, gdn_w_out,
           gla_w_in, gla_w_gate_up, gla_gate_bias, gla_o_norm, gla_w_out,
           mix_pre_norm, mix_post_norm, ffn_pre_norm, ffn_post_norm,
           ffn_w_up, ffn_conv_w, ffn_w_down):
    batch, seq, d = x.shape
    depth = mix_pre_norm.shape[0]
    x2 = x.reshape(batch * seq, d)
    g_pre = _rows(mix_pre_norm)
    ffn_tail = (_rows(mix_post_norm), _rows(ffn_pre_norm), ffn_w_up.astype(BF16), ffn_conv_w,
                ffn_w_down.astype(BF16), _rows(ffn_post_norm))
    gdn_w_in_p = _pad_last(gdn_w_in, GDN_IN_PAD).astype(BF16)
    gla_w_in_p = _pad_last(gla_w_in, GLA_IN_PAD).astype(BF16)
    gdn_w_out_b = gdn_w_out.astype(BF16)
    gla_w_out_b = gla_w_out.astype(BF16)
    on_decay_lanes = lambda v: _rows(jnp.pad(v, ((0, 0), (GDN_HEADS, LANES - 2 * GDN_HEADS))))
    alog_rows = on_decay_lanes(gdn_a_log)
    dtb_rows = on_decay_lanes(gdn_dt_bias)
    wgu = jnp.pad(gla_w_gate_up, ((0, 0), (0, LANES - GLA_GATE_RANK), (0, 0))).astype(BF16)
    for layer in range(depth):
        j = layer // 2
        if layer % 2 == 0:
            x2 = _gdn_layer(x2, g_pre, gdn_w_in_p, gdn_conv_w, alog_rows, dtb_rows, _rows(gdn_o_norm),
                            (gdn_w_out_b,) + ffn_tail, seq, layer, j)
        else:
            x2 = _gla_layer(x2, g_pre, gla_w_in_p, wgu, _rows(gla_gate_bias), _rows(gla_o_norm),
                            (gla_w_out_b,) + ffn_tail, seq, layer, j)
    return x2.reshape(batch, seq, d)
```

```python
import functools

import jax
import jax.numpy as jnp
from jax import lax
from jax.experimental import pallas as pl
from jax.experimental.pallas import tpu as pltpu

EPS = 1e-6
CHUNK = 64
LANES = 128
SUBLANES = 8
MXU_DIM = 256
D_MODEL = 1024

GDN_HEADS = 8
GDN_DK = 128
GDN_DV = 128
GDN_CONV = 4
GDN_QK = GDN_HEADS * GDN_DK
GDN_V = GDN_HEADS * GDN_DV
GDN_PAIRS = GDN_HEADS // 2
GDN_QKV = 2 * GDN_QK + GDN_V
GDN_MAIN = GDN_QKV + GDN_V
GDN_IN_PAD = GDN_MAIN + LANES

GLA_HEADS = 4
GLA_DK = 128
GLA_DV = 256
GLA_GATE_RANK = 16
GLA_TAU = 16.0
GLA_K = GLA_HEADS * GLA_DK
GLA_V = GLA_HEADS * GLA_DV
GLA_MAIN = 2 * GLA_K + 2 * GLA_V
GLA_IN_PAD = GLA_MAIN + LANES
GLA_SUB = 8
GLA_LEVELS = (32, 16, 8)

D_FF = 2816
FFN_CONV = 3
FFN_TILE = MXU_DIM
SEQ_BLOCK = 256
GDN_CHUNK_GROUP = 4
GDN_GROUP_SLOTS = 8
GLA_SLOTS_PER_CHUNK = 4

VMEM_LIMIT = 56 * 1024 * 1024

BF16 = jnp.bfloat16
F32 = jnp.float32


def _dot(a, b):
    return jnp.dot(a.astype(BF16), b.astype(BF16), preferred_element_type=F32)


def _dot_nt(a, b):
    return lax.dot_general(a.astype(BF16), b.astype(BF16), (((1,), (1,)), ((), ())),
                           preferred_element_type=F32)


def _dot_tn(a, b):
    return lax.dot_general(a.astype(BF16), b.astype(BF16), (((0,), (0,)), ((), ())),
                           preferred_element_type=F32)


def _rms(x, g):
    return x * lax.rsqrt(jnp.mean(x * x, axis=-1, keepdims=True) + EPS) * g


def _silu(x):
    return x * jax.nn.sigmoid(x)


def _softplus(x):
    return jnp.maximum(x, 0.0) + jnp.log1p(jnp.exp(-jnp.abs(x)))


def _chunk_cumsum(x):
    n = x.shape[0]
    i = lax.broadcasted_iota(jnp.int32, (n, n), 0)
    j = lax.broadcasted_iota(jnp.int32, (n, n), 1)
    tri = jnp.where((j <= i) & ((i // CHUNK) == (j // CHUNK)), 1.0, 0.0).astype(BF16)
    hi = x.astype(BF16)
    rest = x - hi.astype(F32)
    mid = rest.astype(BF16)
    lo = (rest - mid.astype(F32)).astype(BF16)
    dot = lambda t: jnp.dot(tri, t, preferred_element_type=F32)
    return dot(hi) + dot(mid) + dot(lo)


def _params(*sem):
    return pltpu.CompilerParams(dimension_semantics=sem, vmem_limit_bytes=VMEM_LIMIT)


_FFN_SCRATCH = lambda tm: [
    pltpu.VMEM((2, tm, D_MODEL), BF16),
    pltpu.VMEM((tm, D_MODEL), F32),
    pltpu.VMEM((tm, D_MODEL), F32),
    pltpu.VMEM((tm, D_MODEL), BF16),
    pltpu.VMEM((2, tm + SUBLANES, FFN_TILE), F32),
    pltpu.VMEM((SUBLANES, 2 * D_FF), F32),
    pltpu.VMEM((tm, D_FF), BF16),
    pltpu.VMEM((tm, D_MODEL), F32),
]


def _ffn_steps(o_prev, x_ref, wout_ref, gmix_ref, gpre_ref, wup_ref, fconvw_ref, wdown_ref, gffn_ref,
               y_ref, m_s, xmid_s, h_s, ubuf, fhalo, act_s, f_s, tm):
    steps = []

    def out_proj(n):
        cs = slice(n * MXU_DIM, (n + 1) * MXU_DIM)
        m_s[:, cs] = jnp.dot(o_prev[...], wout_ref[:, cs], preferred_element_type=F32)

    def norms():
        xm = x_ref[...] + _rms(m_s[...], gmix_ref[...])
        xmid_s[...] = xm
        h_s[...] = _rms(xm, gpre_ref[...]).astype(BF16)

    def conv_tile(col, buf):
        u = jnp.dot(h_s[...], wup_ref[:, col:col + FFN_TILE], preferred_element_type=F32)
        ubuf[buf, 0:SUBLANES, :] = fhalo[:, col:col + FFN_TILE]
        ubuf[buf, SUBLANES:SUBLANES + tm, :] = u
        fhalo[:, col:col + FFN_TILE] = u[tm - SUBLANES:tm]
        acc = fconvw_ref[FFN_CONV - 1:FFN_CONV, col:col + FFN_TILE] * u
        for kk in range(FFN_CONV - 1):
            r0 = SUBLANES - (FFN_CONV - 1) + kk
            acc = acc + fconvw_ref[kk:kk + 1, col:col + FFN_TILE] * ubuf[buf, r0:r0 + tm, :]
        return acc

    def up(j):
        gate = conv_tile(j * FFN_TILE, 0)
        val = conv_tile(D_FF + j * FFN_TILE, 1)
        act_s[:, j * FFN_TILE:(j + 1) * FFN_TILE] = (_silu(gate) * val).astype(BF16)

    def down(n):
        cs = slice(n * MXU_DIM, (n + 1) * MXU_DIM)
        f_s[:, cs] = jnp.dot(act_s[...], wdown_ref[:, cs], preferred_element_type=F32)

    def finish():
        y_ref[...] = xmid_s[...] + _rms(f_s[...], gffn_ref[...])

    for n in range(D_MODEL // MXU_DIM):
        steps.append(functools.partial(out_proj, n))
    steps.append(norms)
    for j in range(D_FF // FFN_TILE):
        steps.append(functools.partial(up, j))
    for n in range(D_MODEL // MXU_DIM):
        steps.append(functools.partial(down, n))
    steps.append(finish)
    return steps


class _Interleaver:
    def __init__(self, steps, slots):
        self._steps = list(steps)
        self._total = len(self._steps)
        self._slots = slots
        self._calls = 0

    def one(self):
        self._calls += 1
        due = -(-self._calls * self._total // self._slots)
        while self._steps and self._total - len(self._steps) < due:
            self._steps.pop(0)()

    def rest(self):
        while self._steps:
            self._steps.pop(0)()


def _init_skew(s, nb, o_s, fhalo):
    @pl.when(s == 0)
    def _():
        o_s[...] = jnp.zeros_like(o_s)

    @pl.when((s == 0) | ((s + nb - 1) % nb == 0))
    def _():
        fhalo[...] = jnp.zeros_like(fhalo)


def _inproj_steps(xn_ref, gin_ref, win_ref, hin_s, dst, n_cols, tile):
    def norm():
        hin_s[...] = _rms(xn_ref[...], gin_ref[...]).astype(BF16)

    def tile_step(c0, c1):
        dst[:, c0:c1] = jnp.dot(hin_s[...], win_ref[:, c0:c1], preferred_element_type=F32)

    steps = [norm]
    for c0 in range(0, n_cols, tile):
        steps.append(functools.partial(tile_step, c0, min(c0 + tile, n_cols)))
    return steps


def _spread(steps, extra):
    out, done = [], 0
    for i, step in enumerate(steps):
        out.append(step)
        due = (i + 1) * len(extra) // len(steps)
        out.extend(extra[done:due])
        done = due
    return out


def _first_inproj(s, x_ref, gin_ref, win_ref, p_s, dst0):
    @pl.when(s == 0)
    def _():
        p_s[...] = jnp.zeros_like(p_s)
        dst0[...] = _dot(_rms(x_ref[...], gin_ref[...]), win_ref[...])


def _gdn_layer_kernel(xn_ref, x_ref, gin_ref, win_ref, convw_ref, alog_ref, dtb_ref, ogain_ref,
                      wout_ref, gmix_ref, gpre_ref, wup_ref, fconvw_ref, wdown_ref, gffn_ref,
                      y_ref,
                      p_s, pnext_s, hin_s, q_s, k_s, v_s, beta_s, gc_s, grow_s, state,
                      o_s, m_s, xmid_s, h_s, ubuf, fhalo, act_s, f_s, *, tc, nb):
    nchunks = tc // CHUNK
    s_id = pl.program_id(0)
    cur = s_id % 2
    body = pl.ds(SUBLANES, tc)

    _init_skew(s_id, nb, o_s, fhalo)
    _first_inproj(s_id, x_ref, gin_ref, win_ref, p_s, p_s.at[body, :])

    @pl.when(s_id % nb == 0)
    def _():
        state[...] = jnp.zeros_like(state)
        p_s[0:SUBLANES, :] = jnp.zeros((SUBLANES, GDN_IN_PAD), F32)

    ffn = _Interleaver(_ffn_steps(o_s.at[1 - cur], x_ref, wout_ref, gmix_ref, gpre_ref, wup_ref,
                                  fconvw_ref, wdown_ref, gffn_ref, y_ref,
                                  m_s, xmid_s, h_s, ubuf, fhalo, act_s, f_s, tc),
                       slots=GDN_GROUP_SLOTS * (nchunks // GDN_CHUNK_GROUP))

    def conv_silu(col):
        acc = convw_ref[GDN_CONV - 1:GDN_CONV, col:col + LANES] * p_s[body, col:col + LANES]
        for kk in range(GDN_CONV - 1):
            r0 = SUBLANES - (GDN_CONV - 1) + kk
            acc = acc + convw_ref[kk:kk + 1, col:col + LANES] * p_s[pl.ds(r0, tc), col:col + LANES]
        return _silu(acc)

    def l2n(t):
        return t * lax.rsqrt(jnp.sum(t * t, axis=-1, keepdims=True) + EPS)

    for h in range(GDN_HEADS):
        c0 = h * GDN_DK
        q_s[:, c0:c0 + GDN_DK] = l2n(conv_silu(c0)) * (GDN_DK ** -0.5)
        k_s[:, c0:c0 + GDN_DK] = l2n(conv_silu(GDN_QK + c0))
        v_s[:, c0:c0 + GDN_DV] = conv_silu(2 * GDN_QK + h * GDN_DV)
    for step in _inproj_steps(xn_ref, gin_ref, win_ref, hin_s, pnext_s, GDN_IN_PAD, MXU_DIM):
        step()

    pg = p_s[body, GDN_MAIN:GDN_MAIN + LANES]
    lane = lax.broadcasted_iota(jnp.int32, (tc, LANES), 1)
    beta_s[...] = jax.nn.sigmoid(pg)
    g = -jnp.exp(alog_ref[...]) * _softplus(pg + dtb_ref[...])
    g = jnp.where((lane >= GDN_HEADS) & (lane < 2 * GDN_HEADS), g, 0.0)
    gc = _chunk_cumsum(g)
    gc_s[...] = gc
    gct = gc.T
    lane_row = lax.broadcasted_iota(jnp.int32, (1, LANES), 1)
    for m in range(tc // LANES):
        tile = gct[GDN_HEADS:2 * GDN_HEADS, m * LANES:(m + 1) * LANES]
        swapped = pltpu.roll(tile, CHUNK, axis=1)
        for pr in range(GDN_PAIRS):
            a, b = 2 * pr, 2 * pr + 1
            grow_s[pl.ds((2 * m) * GDN_PAIRS + pr, 1), :] = jnp.where(
                lane_row < CHUNK, tile[a:a + 1], swapped[b:b + 1])
            grow_s[pl.ds((2 * m + 1) * GDN_PAIRS + pr, 1), :] = jnp.where(
                lane_row < CHUNK, swapped[a:a + 1], tile[b:b + 1])

    ii = lax.broadcasted_iota(jnp.int32, (CHUNK, 2 * CHUNK), 0)
    ll = lax.broadcasted_iota(jnp.int32, (CHUNK, 2 * CHUNK), 1)
    first = ll < CHUNK
    jj = jnp.where(first, ll, ll - CHUNK)
    causal = ii >= jj
    strict = ii > jj
    eye = jnp.where(ii == jj, 1.0, 0.0).astype(F32)
    heads = range(GDN_HEADS)
    pairs = range(GDN_PAIRS)
    hsl = [slice(h * GDN_DK, (h + 1) * GDN_DK) for h in heads]

    def blockdiag(x):
        return jnp.concatenate([jnp.where(first, x, 0.0), jnp.where(first, 0.0, x)], axis=0)

    def blockdiag_wide(xa, xb):
        z = jnp.zeros_like(xa)
        return jnp.concatenate([jnp.concatenate([xa, z], axis=1), jnp.concatenate([z, xb], axis=1)], axis=0)

    def rows_of(c):
        return pl.ds(c * CHUNK, CHUNK)

    def head_cols(ref, c, h):
        return ref[rows_of(c), hsl[h]]

    def gcol(c, h):
        return gc_s[rows_of(c), GDN_HEADS + h:GDN_HEADS + h + 1]

    def kbeta(c, h):
        return head_cols(k_s, c, h) * beta_s[rows_of(c), h:h + 1]

    for g0 in range(0, nchunks, GDN_CHUNK_GROUP):
        units = [(c, pr) for c in range(g0, g0 + GDN_CHUNK_GROUP) for pr in pairs]
        decay = [jnp.exp(jnp.where(causal, jnp.where(first, gcol(c, 2 * pr), gcol(c, 2 * pr + 1))
                                   - grow_s[pl.ds(c * GDN_PAIRS + pr, 1), :], -jnp.inf)) for c, pr in units]
        kk = [_dot_nt(jnp.concatenate(
                  [jnp.concatenate([kbeta(c, 2 * pr), kbeta(c, 2 * pr + 1)], axis=1),
                   jnp.concatenate([head_cols(q_s, c, 2 * pr), head_cols(q_s, c, 2 * pr + 1)], axis=1)], axis=0),
                  blockdiag_wide(head_cols(k_s, c, 2 * pr), head_cols(k_s, c, 2 * pr + 1))) for c, pr in units]
        ffn.one()
        nu = range(len(units))
        a = [jnp.where(strict, kk[u][:CHUNK] * decay[u], 0.0) for u in nu]
        qk = [jnp.where(causal, kk[u][CHUNK:] * decay[u], 0.0) for u in nu]
        tinv = [eye - a[u] for u in nu]
        apow = [_dot(a[u], blockdiag(a[u])) for u in nu]
        ffn.one()
        for _ in range(4):
            r = [_dot(jnp.concatenate([apow[u], tinv[u]], axis=0), blockdiag(apow[u])) for u in nu]
            ffn.one()
            apow = [r[u][:CHUNK] for u in nu]
            tinv = [tinv[u] + r[u][CHUNK:] for u in nu]
        tinv = [tinv[u] + _dot(tinv[u], blockdiag(apow[u])) for u in nu]
        ffn.one()

        def rhs(c, h):
            return jnp.concatenate([head_cols(v_s, c, h) * beta_s[rows_of(c), h:h + 1],
                                    kbeta(c, h) * jnp.exp(gcol(c, h))], axis=-1)

        sol_p = [_dot(tinv[u], blockdiag_wide(rhs(c, 2 * pr), rhs(c, 2 * pr + 1)))
                 for u, (c, pr) in enumerate(units)]
        ffn.one()
        width = GDN_DV + GDN_DK
        for ci, c in enumerate(range(g0, g0 + GDN_CHUNK_GROUP)):
            rows = rows_of(c)
            sol = [sol_p[ci * GDN_PAIRS + h // 2][:, (h % 2) * width:(h % 2 + 1) * width] for h in heads]
            s = [state[h] for h in heads]
            ws = [_dot(jnp.concatenate([sol[h][:, GDN_DV:], head_cols(q_s, c, h) * jnp.exp(gcol(c, h))], axis=0),
                       s[h]) for h in heads]
            v_new = [sol[h][:, :GDN_DV] - ws[h][:CHUNK] for h in heads]
            for h in heads:
                glast = gc_s[pl.ds(c * CHUNK + CHUNK - 1, 1), GDN_HEADS + h:GDN_HEADS + h + 1]
                k_dec = head_cols(k_s, c, h) * jnp.exp(glast - gcol(c, h))
                state[h] = s[h] * jnp.exp(glast) + _dot_tn(k_dec, v_new[h])
            o_p = [_dot(qk[ci * GDN_PAIRS + pr], blockdiag_wide(v_new[2 * pr], v_new[2 * pr + 1])) for pr in pairs]
            for h in heads:
                o = ws[h][CHUNK:] + o_p[h // 2][:, (h % 2) * GDN_DV:(h % 2 + 1) * GDN_DV]
                z = p_s[pl.ds(SUBLANES + c * CHUNK, CHUNK), GDN_QKV + h * GDN_DV:GDN_QKV + (h + 1) * GDN_DV]
                o_s[cur, rows, hsl[h]] = (_rms(o, ogain_ref[...]) * _silu(z)).astype(BF16)
    ffn.rest()
    p_s[0:SUBLANES, :] = p_s[tc:tc + SUBLANES, :]
    p_s[body, :] = pnext_s[...]


def _stacked(rows, cols, index, resident=False):
    mode = dict(pipeline_mode=pl.Buffered(1)) if resident else {}
    return pl.BlockSpec((None, rows, cols), lambda s: (index, 0, 0), **mode)


def _layer_specs(tc, nblk, p_cols, layer, j):
    prev = lambda s: (jnp.maximum(s - 1, 0), 0)
    x_specs = [pl.BlockSpec((tc, D_MODEL), lambda s: (jnp.minimum(s + 1, nblk - 1), 0)),
               pl.BlockSpec((tc, D_MODEL), prev),
               _stacked(1, D_MODEL, layer),
               _stacked(D_MODEL, p_cols, j, resident=True)]
    ffn_specs = [_stacked(D_MODEL, D_MODEL, j, resident=True),
                 _stacked(1, D_MODEL, layer),
                 _stacked(1, D_MODEL, layer),
                 _stacked(D_MODEL, 2 * D_FF, layer, resident=True),
                 _stacked(FFN_CONV, 2 * D_FF, layer),
                 _stacked(D_FF, D_MODEL, layer, resident=True),
                 _stacked(1, D_MODEL, layer)]
    y_spec = pl.BlockSpec((tc, D_MODEL), prev)
    return x_specs, ffn_specs, y_spec


def _gdn_layer(x2, g_in, w_in, conv_w, alog_rows, dtb_rows, o_gain, ffn_args, seq, layer, j, tc=SEQ_BLOCK):
    t = x2.shape[0]
    nb = seq // tc
    nblk = t // tc
    x_specs, ffn_specs, y_spec = _layer_specs(tc, nblk, GDN_IN_PAD, layer, j)
    return pl.pallas_call(
        functools.partial(_gdn_layer_kernel, tc=tc, nb=nb),
        grid=(nblk + 1,),
        in_specs=x_specs + [
                  _stacked(GDN_CONV, GDN_QKV, j),
                  _stacked(1, LANES, j),
                  _stacked(1, LANES, j),
                  _stacked(1, GDN_DV, j)] + ffn_specs,
        out_specs=y_spec,
        out_shape=jax.ShapeDtypeStruct((t, D_MODEL), F32),
        scratch_shapes=[pltpu.VMEM((tc + SUBLANES, GDN_IN_PAD), F32),
                        pltpu.VMEM((tc, GDN_IN_PAD), F32),
                        pltpu.VMEM((tc, D_MODEL), BF16),
                        pltpu.VMEM((tc, GDN_QK), F32),
                        pltpu.VMEM((tc, GDN_QK), F32),
                        pltpu.VMEM((tc, GDN_V), F32),
                        pltpu.VMEM((tc, LANES), F32),
                        pltpu.VMEM((tc, LANES), F32),
                        pltpu.VMEM((tc // CHUNK * GDN_PAIRS, LANES), F32),
                        pltpu.VMEM((GDN_HEADS, GDN_DK, GDN_DV), F32),
                        ] + _FFN_SCRATCH(tc),
        compiler_params=_params("arbitrary"),
        name="gdn_layer",
    )(x2, x2, g_in, w_in, conv_w, alog_rows, dtb_rows, o_gain, *ffn_args)


def _gla_layer_kernel(xn_ref, x_ref, gin_ref, win_ref, wgu_ref, gbias_ref, ogain_ref,
                      wout_ref, gmix_ref, gpre_ref, wup_ref, fconvw_ref, wdown_ref, gffn_ref,
                      y_ref,
                      p_ref, pnext_s, hin_s, bc_s, state,
                      o_s, m_s, xmid_s, h_s, ubuf, fhalo, act_s, f_s, *, tc, nb):
    nchunks = tc // CHUNK
    s_id = pl.program_id(0)
    cur = s_id % 2

    @pl.when(s_id % nb == 0)
    def _():
        state[...] = jnp.zeros_like(state)

    _init_skew(s_id, nb, o_s, fhalo)
    _first_inproj(s_id, x_ref, gin_ref, win_ref, p_ref, p_ref)
    ffn = _Interleaver(
        _spread(_ffn_steps(o_s.at[1 - cur], x_ref, wout_ref, gmix_ref, gpre_ref, wup_ref,
                           fconvw_ref, wdown_ref, gffn_ref, y_ref,
                           m_s, xmid_s, h_s, ubuf, fhalo, act_s, f_s, tc),
                _inproj_steps(xn_ref, gin_ref, win_ref, hin_s, pnext_s, GLA_IN_PAD, MXU_DIM)),
        slots=2 + GLA_SLOTS_PER_CHUNK * nchunks)

    z = _dot(p_ref[:, GLA_MAIN:GLA_MAIN + LANES], wgu_ref[...]) + gbias_ref[...]
    log_a = -_softplus(-z) * (1.0 / GLA_TAU)
    ffn.one()
    bc_s[...] = _chunk_cumsum(log_a)
    ffn.one()

    ii = lax.broadcasted_iota(jnp.int32, (CHUNK, CHUNK), 0)
    jj = lax.broadcasted_iota(jnp.int32, (CHUNK, CHUNK), 1)
    level_masks = [((ii // lv) == (jj // lv) + 1) & (((ii // lv) % 2) == 1) for lv in GLA_LEVELS]
    isub = lax.broadcasted_iota(jnp.int32, (GLA_SUB, GLA_DK), 0)
    jsub = lax.broadcasted_iota(jnp.int32, (GLA_SUB, CHUNK), 1)
    scale = GLA_DK ** -0.5
    heads = range(GLA_HEADS)

    def block_rows(b, lv, shift):
        parts = []
        for m in range(CHUNK // lv):
            r = min((m + shift) * lv, CHUNK - 1)
            parts.append(jnp.broadcast_to(b[r:r + 1], (lv, GLA_DK)))
        return jnp.concatenate(parts, axis=0)

    for c in range(nchunks):
        rows = pl.ds(c * CHUNK, CHUNK)
        q = [p_ref[rows, h * GLA_DK:(h + 1) * GLA_DK] * scale for h in heads]
        k = [p_ref[rows, GLA_K + h * GLA_DK:GLA_K + (h + 1) * GLA_DK] for h in heads]
        b = [bc_s[rows, h * GLA_DK:(h + 1) * GLA_DK] for h in heads]
        attn = [jnp.zeros((CHUNK, CHUNK), F32) for h in heads]
        for lv, mask in zip(GLA_LEVELS, level_masks):
            q_t = [q[h] * jnp.exp(b[h] - block_rows(b[h], lv, 0)) for h in heads]
            k_t = [k[h] * jnp.exp(block_rows(b[h], lv, 1) - b[h]) for h in heads]
            attn = [jnp.where(mask, _dot_nt(q_t[h], k_t[h]), attn[h]) for h in heads]
            ffn.one()
        diag = []
        for h in heads:
            blocks = []
            for blk in range(CHUNK // GLA_SUB):
                i0 = blk * GLA_SUB
                q_i = q[h][i0:i0 + GLA_SUB]
                b_i = b[h][i0:i0 + GLA_SUB]
                k_i = k[h][i0:i0 + GLA_SUB]
                acc = jnp.zeros((GLA_SUB, CHUNK), F32)
                for j in range(GLA_SUB):
                    e = jnp.exp(jnp.where(isub >= j, b_i - b_i[j:j + 1], -jnp.inf))
                    col = jnp.sum(q_i * k_i[j:j + 1] * e, axis=-1, keepdims=True)
                    acc = jnp.where(jsub == i0 + j, col, acc)
                blocks.append(acc)
            diag.append(jnp.concatenate(blocks, axis=0))
        st = [state[h] for h in heads]
        o = []
        for h in heads:
            v = p_ref[rows, 2 * GLA_K + h * GLA_DV:2 * GLA_K + (h + 1) * GLA_DV]
            blast = b[h][CHUNK - 1:CHUNK, :]
            o.append(_dot(jnp.concatenate([q[h] * jnp.exp(b[h]), attn[h] + diag[h]], axis=1),
                          jnp.concatenate([st[h], v], axis=0)))
            decay_col = jnp.transpose(jnp.broadcast_to(jnp.exp(blast), (SUBLANES, GLA_DK)))[:, 0:1]
            state[h] = st[h] * decay_col + _dot_tn(k[h] * jnp.exp(blast - b[h]), v)
        ffn.one()
        for h in heads:
            r = p_ref[rows, 2 * GLA_K + GLA_V + h * GLA_DV:2 * GLA_K + GLA_V + (h + 1) * GLA_DV]
            o_s[cur, rows, h * GLA_DV:(h + 1) * GLA_DV] = (
                _rms(o[h], ogain_ref[...]) * _silu(r)).astype(BF16)
    ffn.rest()
    p_ref[...] = pnext_s[...]


def _gla_layer(x2, g_in, w_in, wgu, gate_bias, o_gain, ffn_args, seq, layer, j, tc=SEQ_BLOCK):
    t = x2.shape[0]
    nb = seq // tc
    nblk = t // tc
    x_specs, ffn_specs, y_spec = _layer_specs(tc, nblk, GLA_IN_PAD, layer, j)
    return pl.pallas_call(
        functools.partial(_gla_layer_kernel, tc=tc, nb=nb),
        grid=(nblk + 1,),
        in_specs=x_specs + [
                  _stacked(LANES, GLA_K, j),
                  _stacked(1, GLA_K, j),
                  _stacked(1, GLA_DV, j)] + ffn_specs,
        out_specs=y_spec,
        out_shape=jax.ShapeDtypeStruct((t, D_MODEL), F32),
        scratch_shapes=[pltpu.VMEM((tc, GLA_IN_PAD), F32),
                        pltpu.VMEM((tc, GLA_IN_PAD), F32),
                        pltpu.VMEM((tc, D_MODEL), BF16),
                        pltpu.VMEM((tc, GLA_K), F32),
                        pltpu.VMEM((GLA_HEADS, GLA_DK, GLA_DV), F32),
                        ] + _FFN_SCRATCH(tc),
        compiler_params=_params("arbitrary"),
        name="gla_layer",
    )(x2, x2, g_in, w_in, wgu, gate_bias, o_gain, *ffn_args)


def _pad_last(w, n):
    return jnp.pad(w, [(0, 0)] * (w.ndim - 1) + [(0, n - w.shape[-1])])


def _rows(v):
    return v[:, None, :]


def kernel(x, gdn_w_in, gdn_conv_w, gdn_a_log, gdn_dt_bias, gdn_o_norm, gdn_w_out,
           gla_w_in, gla_w_gate_up, gla_gate_bias, gla_o_norm, gla_w_out,
           mix_pre_norm, mix_post_norm, ffn_pre_norm, ffn_post_norm,
           ffn_w_up, ffn_conv_w, ffn_w_down):
    batch, seq, d = x.shape
    depth = mix_pre_norm.shape[0]
    x2 = x.reshape(batch * seq, d)
    g_pre = _rows(mix_pre_norm)
    ffn_tail = (_rows(mix_post_norm), _rows(ffn_pre_norm), ffn_w_up.astype(BF16), ffn_conv_w,
                ffn_w_down.astype(BF16), _rows(ffn_post_norm))
    gdn_w_in_p = _pad_last(gdn_w_in, GDN_IN_PAD).astype(BF16)
    gla_w_in_p = _pad_last(gla_w_in, GLA_IN_PAD).astype(BF16)
    gdn_w_out_b = gdn_w_out.astype(BF16)
    gla_w_out_b = gla_w_out.astype(BF16)
    on_decay_lanes = lambda v: _rows(jnp.pad(v, ((0, 0), (GDN_HEADS, LANES - 2 * GDN_HEADS))))
    alog_rows = on_decay_lanes(gdn_a_log)
    dtb_rows = on_decay_lanes(gdn_dt_bias)
    wgu = jnp.pad(gla_w_gate_up, ((0, 0), (0, LANES - GLA_GATE_RANK), (0, 0))).astype(BF16)
    for layer in range(depth):
        j = layer // 2
        if layer % 2 == 0:
            x2 = _gdn_layer(x2, g_pre, gdn_w_in_p, gdn_conv_w, alog_rows, dtb_rows, _rows(gdn_o_norm),
                            (gdn_w_out_b,) + ffn_tail, seq, layer, j)
        else:
            x2 = _gla_layer(x2, g_pre, gla_w_in_p, wgu, _rows(gla_gate_bias), _rows(gla_o_norm),
                            (gla_w_out_b,) + ffn_tail, seq, layer, j)
    return x2.reshape(batch, seq, d)
```
